```python
import jax, jax.numpy as jnp
from jax import lax
import numpy as np


D_MODEL = 2048
BATCH = 2
SEQ = 4096
DEPTH = 1

EPS = 1e-6
HEAD_DIM = 64
ATTN_WIDTH = D_MODEL // 2
N_HEADS_A = ATTN_WIDTH // HEAD_DIM
N_KV_HEADS = 4
GROUP_SIZE = N_HEADS_A // N_KV_HEADS
KV_WIDTH = N_KV_HEADS * HEAD_DIM
IDX_HEADS = 16
IDX_DIM = 64
TOPK_MAX = 256
Q_BLOCK = 128
CONV_WIDTH = D_MODEL // 2
CONV_K = 3
N_GROUPS = 4
EXPERTS_PER_GROUP = 8
N_EXPERTS = N_GROUPS * EXPERTS_PER_GROUP
TOP_K_IN_GROUP = 2
EXPERT_FF = 512
TOKEN_CHUNK = 128
N_MOD = 6
PROJ_SIZES = (ATTN_WIDTH, KV_WIDTH, KV_WIDTH, IDX_HEADS * IDX_DIM, IDX_DIM, IDX_HEADS,
              CONV_WIDTH, CONV_WIDTH, CONV_WIDTH, D_MODEL, D_MODEL)
PROJ_WIDTH = sum(PROJ_SIZES)

kernel_name = 'hybrid_dsa_shortconv_hmoe_block'


def _rmsnorm(x, g):
    xf = x.astype(jnp.float32)
    r = lax.rsqrt(jnp.mean(xf * xf, axis=-1, keepdims=True) + EPS)
    return (xf * r).astype(x.dtype) * g


def _sparse_attention(q, k, v, iq, ik, iw):
    b, s = q.shape[0], q.shape[1]
    k_top = min(TOPK_MAX, s // 4)
    nb = s // Q_BLOCK
    neg = jnp.finfo(jnp.float32).min
    key_pos = jnp.arange(s)
    ik32 = ik.astype(jnp.float32)

    def to_blocks(a):
        return jnp.moveaxis(a.reshape((b, nb, Q_BLOCK) + a.shape[2:]), 1, 0)

    def block(args):
        qb, iqb, iwb, t0 = args
        q_pos = t0 + jnp.arange(Q_BLOCK)
        rel = jax.nn.relu(jnp.einsum('bqhd,bsd->bqhs', iqb.astype(jnp.float32), ik32) * (IDX_DIM ** -0.5))
        score = jnp.einsum('bqhs,bqh->bqs', rel, iwb.astype(jnp.float32) * (IDX_HEADS ** -0.5))
        causal = key_pos[None, :] <= q_pos[:, None]
        score = jnp.where(causal[None], score, neg)
        _, sel = lax.top_k(score, k_top)
        k_sel = jax.vmap(lambda kb, ib: kb[ib])(k, sel)
        v_sel = jax.vmap(lambda vb, ib: vb[ib])(v, sel)
        qg = qb.reshape(b, Q_BLOCK, N_KV_HEADS, GROUP_SIZE, HEAD_DIM)
        logits = jnp.einsum('bqgrd,bqkgd->bqgrk', qg, k_sel).astype(jnp.float32) * (HEAD_DIM ** -0.5)
        valid = (sel <= q_pos[None, :, None])[:, :, None, None, :]
        p = jax.nn.softmax(jnp.where(valid, logits, neg), axis=-1).astype(v.dtype)
        o = jnp.einsum('bqgrk,bqkgd->bqgrd', p, v_sel)
        return o.reshape(b, Q_BLOCK, N_HEADS_A * HEAD_DIM)

    starts = jnp.arange(nb) * Q_BLOCK
    out = lax.map(block, (to_blocks(q), to_blocks(iq), to_blocks(iw), starts))
    return jnp.moveaxis(out, 0, 1).reshape(b, s, N_HEADS_A * HEAD_DIM)


def _short_conv(u, w):
    s = u.shape[1]
    pad = jnp.pad(u, ((0, 0), (CONV_K - 1, 0), (0, 0)))
    y = w[0] * pad[:, 0:s]
    for j in range(1, CONV_K):
        y = y + w[j] * pad[:, j:j + s]
    return y


def _hybrid_mixer(h, w_in, conv_w, w_a_up, w_b_out, w_o):
    b, s, _ = h.shape
    proj = h @ w_in
    points = np.cumsum(PROJ_SIZES)[:-1].tolist()
    q, k, v, iq, ik, iw, bg, cg, xc, gate_a, gate_b = jnp.split(proj, points, axis=-1)
    o_a = _sparse_attention(q.reshape(b, s, N_HEADS_A, HEAD_DIM),
                            k.reshape(b, s, N_KV_HEADS, HEAD_DIM),
                            v.reshape(b, s, N_KV_HEADS, HEAD_DIM),
                            iq.reshape(b, s, IDX_HEADS, IDX_DIM), ik, iw)
    y_a = o_a @ w_a_up
    y_b = (bg * _short_conv(cg * xc, conv_w)) @ w_b_out
    merged = jax.nn.sigmoid(gate_a) * y_a + jax.nn.sigmoid(gate_b) * y_b
    return merged @ w_o


def _hier_moe(h, w_rg, b_rg, w_re, b_re, w_gate, w_up, w_down):
    b, s, d = h.shape
    t = h.reshape(b * s, d)
    g_logits = (t @ w_rg + b_rg).astype(jnp.float32)
    p_group = jax.nn.softmax(g_logits, axis=-1)
    g_sel = jnp.argmax(g_logits, axis=-1)
    e_logits = (t @ w_re + b_re).astype(jnp.float32).reshape(-1, N_GROUPS, EXPERTS_PER_GROUP)
    within = jnp.take_along_axis(e_logits, g_sel[:, None, None], axis=1)[:, 0]
    top_v, top_e = lax.top_k(within, TOP_K_IN_GROUP)
    p_in = jax.nn.softmax(top_v, axis=-1)
    p_g = jnp.take_along_axis(p_group, g_sel[:, None], axis=1)
    weights = p_g * p_in
    eid = g_sel[:, None] * EXPERTS_PER_GROUP + top_e
    combine = jnp.einsum('nk,nke->ne', weights,
                         jax.nn.one_hot(eid, N_EXPERTS, dtype=jnp.float32)).astype(h.dtype)
    nc = t.shape[0] // TOKEN_CHUNK

    def chunk(args):
        tc, cc = args
        a = jnp.einsum('nd,edf->nef', tc, w_gate)
        u = jnp.einsum('nd,edf->nef', tc, w_up)
        act = jax.nn.silu(a) * u * cc[:, :, None]
        return jnp.einsum('nef,efd->nd', act, w_down)

    out = lax.map(chunk, (t.reshape(nc, TOKEN_CHUNK, d), combine.reshape(nc, TOKEN_CHUNK, N_EXPERTS)))
    return out.reshape(b, s, d)


def _normal(key, shape, scale):
    return jax.random.normal(key, shape, jnp.float32) * scale


def setup_inputs(seed: int = 0) -> dict:
    key = jax.random.key(seed)
    ks = jax.random.split(key, 19)
    D = D_MODEL
    return {
        'x': _normal(ks[0], (BATCH, SEQ, D), 1.0),
        'c': _normal(ks[1], (BATCH, D), 1.0),
        'w_ada': _normal(ks[2], (DEPTH, D, N_MOD * D), D ** -0.5),
        'b_ada': _normal(ks[3], (DEPTH, N_MOD * D), 0.02),
        'g_norm_mix': 1.0 + _normal(ks[4], (DEPTH, D), 0.05),
        'w_in': _normal(ks[5], (DEPTH, D, PROJ_WIDTH), D ** -0.5),
        'conv_w': _normal(ks[6], (DEPTH, CONV_K, CONV_WIDTH), CONV_K ** -0.5),
        'w_a_up': _normal(ks[7], (DEPTH, ATTN_WIDTH, D), ATTN_WIDTH ** -0.5),
        'w_b_out': _normal(ks[8], (DEPTH, CONV_WIDTH, D), CONV_WIDTH ** -0.5),
        'w_o': _normal(ks[9], (DEPTH, D, D), D ** -0.5),
        'g_norm_ffn': 1.0 + _normal(ks[10], (DEPTH, D), 0.05),
        'w_rg': _normal(ks[11], (DEPTH, D, N_GROUPS), D ** -0.5),
        'b_rg': _normal(ks[12], (DEPTH, N_GROUPS), 0.01),
        'w_re': _normal(ks[13], (DEPTH, D, N_EXPERTS), D ** -0.5),
        'b_re': _normal(ks[14], (DEPTH, N_EXPERTS), 0.01),
        'w_e_gate': _normal(ks[15], (DEPTH, N_EXPERTS, D, EXPERT_FF), D ** -0.5),
        'w_e_up': _normal(ks[16], (DEPTH, N_EXPERTS, D, EXPERT_FF), D ** -0.5),
        'w_e_down': _normal(ks[17], (DEPTH, N_EXPERTS, EXPERT_FF, D), EXPERT_FF ** -0.5),
        'g_norm_final': 1.0 + _normal(ks[18], (D,), 0.05),
    }


def reference(x, c, w_ada, b_ada, g_norm_mix, w_in, conv_w, w_a_up, w_b_out, w_o,
              g_norm_ffn, w_rg, b_rg, w_re, b_re, w_e_gate, w_e_up, w_e_down, g_norm_final):
    c_act = jax.nn.silu(c)
    for l in range(DEPTH):
        mod = (c_act @ w_ada[l] + b_ada[l])[:, None, :]
        sh1, sc1, g1, sh2, sc2, g2 = jnp.split(mod, N_MOD, axis=-1)
        h = _rmsnorm(x, g_norm_mix[l]) * (1.0 + sc1) + sh1
        x = x + g1 * _hybrid_mixer(h, w_in[l], conv_w[l], w_a_up[l], w_b_out[l], w_o[l])
        h = _rmsnorm(x, g_norm_ffn[l]) * (1.0 + sc2) + sh2
        x = x + g2 * _hier_moe(h, w_rg[l], b_rg[l], w_re[l], b_re[l],
                               w_e_gate[l], w_e_up[l], w_e_down[l])
    return _rmsnorm(x, g_norm_final)
```

```python
import functools

import jax
import jax.numpy as jnp
from jax import lax
from jax.experimental import pallas as pl
from jax.experimental.pallas import tpu as pltpu

F32 = jnp.float32
BF16 = jnp.bfloat16

D = 2048
B = 2
S = 4096
N = B * S
EPS = 1e-6
HEAD_DIM = 64
ATTN_W = 1024
N_KV = 4
KV_W = 256
IDX_HEADS = 16
IDX_DIM = 64
TOPK = min(256, S // 4)
CONV_W = 1024
N_GROUPS = 4
EPG = 8
N_EXP = 32
FF = 512
N_MOD = 6

LANES = 128
TQ = 128
TK = 512
CH = 256
MAX_CHUNKS = (N * 2) // CH + N_EXP
XS_ROWS = MAX_CHUNKS * CH
MASKED = -1e30
BISECT_MAX_IT = 64
VMEM_LIMIT = 56 * 1024 * 1024


def _dot(a, b):
    return jnp.dot(a, b, preferred_element_type=F32)


def _dot_nt(a, b):
    return lax.dot_general(a, b, (((1,), (1,)), ((), ())), preferred_element_type=F32)


def _params(*sem):
    return pltpu.CompilerParams(dimension_semantics=sem, vmem_limit_bytes=VMEM_LIMIT)


ADA_TN = 1024
ADA_KC = 256


def _ada_kernel(ct_ref, w_ref, b_ref, o_ref):
    tn = w_ref.shape[1]

    def body(i, accs):
        k0 = pl.multiple_of(i * ADA_KC, ADA_KC)
        ct = ct_ref[pl.ds(k0, ADA_KC), :]
        act = ct * jax.nn.sigmoid(ct)
        w = w_ref[pl.ds(k0, ADA_KC), :]
        out = []
        for b in range(B):
            prod = w * act[:, b:b + 1]
            out.append(accs[b] + prod.reshape(ADA_KC // 8, 8, tn).sum(axis=0))
        return tuple(out)

    accs = lax.fori_loop(0, D // ADA_KC, body, tuple(jnp.zeros((8, tn), F32) for _ in range(B)))
    rows = [jnp.sum(a, axis=0, keepdims=True) for a in accs]
    o_ref[...] = jnp.concatenate(rows, axis=0) + b_ref[...]


def _ada(c, w, b):
    n_out = w.shape[1]
    return pl.pallas_call(
        _ada_kernel,
        grid=(n_out // ADA_TN,),
        in_specs=[pl.BlockSpec((D, B), lambda j: (0, 0)),
                  pl.BlockSpec((D, ADA_TN), lambda j: (0, j)),
                  pl.BlockSpec((1, ADA_TN), lambda j: (0, j))],
        out_specs=pl.BlockSpec((B, ADA_TN), lambda j: (0, j)),
        out_shape=jax.ShapeDtypeStruct((B, n_out), F32),
        compiler_params=_params("arbitrary"),
        name="ada_mod",
    )(c.T, w, b.reshape(1, n_out))


NORM_TM = 512


def _modulated_norm(x, g, shift, scale):
    r = lax.rsqrt(jnp.mean(x * x, axis=-1, keepdims=True) + EPS)
    return (x * r) * g * (1.0 + scale) + shift


def _norm1_kernel(x_ref, g_ref, mod_ref, h_ref):
    h = _modulated_norm(x_ref[...], g_ref[...], mod_ref[0:1, :], mod_ref[1:2, :])
    h_ref[...] = h.astype(h_ref.dtype)


def _norm1(x2, g, mod3):
    tpb = S // NORM_TM
    return pl.pallas_call(
        _norm1_kernel,
        grid=(N // NORM_TM,),
        in_specs=[pl.BlockSpec((NORM_TM, D), lambda i: (i, 0)),
                  pl.BlockSpec((1, D), lambda i: (0, 0)),
                  pl.BlockSpec((None, N_MOD, D), lambda i: (i // tpb, 0, 0))],
        out_specs=pl.BlockSpec((NORM_TM, D), lambda i: (i, 0)),
        out_shape=jax.ShapeDtypeStruct((N, D), BF16),
        compiler_params=_params("arbitrary"),
        name="norm1",
    )(x2, g.reshape(1, D), mod3)


def _mm_kernel(a_ref, w_ref, o_ref):
    o_ref[...] = _dot(a_ref[...], w_ref[...]).astype(o_ref.dtype)


def _matmul(a, w, out_dtype, tm, tn, name):
    m, k = a.shape
    n = w.shape[1]
    return pl.pallas_call(
        _mm_kernel,
        grid=(m // tm, n // tn),
        in_specs=[pl.BlockSpec((tm, k), lambda i, j: (i, 0)),
                  pl.BlockSpec((k, tn), lambda i, j: (0, j))],
        out_specs=pl.BlockSpec((tm, tn), lambda i, j: (i, j)),
        out_shape=jax.ShapeDtypeStruct((m, n), out_dtype),
        compiler_params=_params("arbitrary", "arbitrary"),
        name=name,
    )(a, w)


A_COLS = 2 * ATTN_W + 4 * KV_W + KV_W + 4 * IDX_DIM


def _attn_kernel(q_ref, iq_ref, iw_ref, ikp_ref, kp_ref, v_ref, o_ref,
                 s_ref, thr_ref, qs_ref, iqs_ref, m_ref, l_ref, acc_ref):
    qi = pl.program_id(1)
    nkc = qi // (TK // TQ) + 1
    rowg = qi * TQ + lax.broadcasted_iota(jnp.int32, (TQ, TK), 0)
    coll = lax.broadcasted_iota(jnp.int32, (TQ, TK), 1)

    for j in range(8):
        iqs_ref[j * TQ:(j + 1) * TQ, :] = iq_ref[:, LANES * j:LANES * (j + 1)]
        qs_ref[j * TQ:(j + 1) * TQ, :] = q_ref[:, LANES * j:LANES * (j + 1)] * (HEAD_DIM ** -0.5)

    w = iw_ref[:, 0:IDX_HEADS] * ((IDX_HEADS ** -0.5) * (IDX_DIM ** -0.5))

    def score_body(c, carry):
        lo, hi = carry
        off = pl.multiple_of(c * TK, TK)
        ikc = ikp_ref[pl.ds(off, TK), :]
        iqs = iqs_ref[...]
        d_even = _dot_nt(iqs, ikc[:, 0:LANES])
        d_odd = _dot_nt(iqs, ikc[:, LANES:2 * LANES])
        acc = jnp.zeros((TQ, TK), F32)
        for j in range(8):
            acc = acc + jnp.maximum(d_even[j * TQ:(j + 1) * TQ], 0.0) * w[:, 2 * j:2 * j + 1]
            acc = acc + jnp.maximum(d_odd[j * TQ:(j + 1) * TQ], 0.0) * w[:, 2 * j + 1:2 * j + 2]
        valid = (off + coll) <= rowg
        s_ref[:, pl.ds(off, TK)] = jnp.where(valid, acc, -jnp.inf)
        hi = jnp.maximum(hi, jnp.max(jnp.where(valid, acc, -jnp.inf), axis=1, keepdims=True))
        lo = jnp.minimum(lo, jnp.min(jnp.where(valid, acc, jnp.inf), axis=1, keepdims=True))
        return lo, hi

    lo0, hi0 = lax.fori_loop(0, nkc, score_body,
                             (jnp.full((TQ, 1), jnp.inf, F32), jnp.full((TQ, 1), -jnp.inf, F32)))

    def count_ge(mid):
        midb = jnp.broadcast_to(mid, (TQ, LANES))

        def body(c, cnt):
            off = pl.multiple_of(c * TK, TK)
            s = s_ref[:, pl.ds(off, TK)]
            for i in range(TK // LANES):
                cnt = cnt + jnp.where(s[:, LANES * i:LANES * (i + 1)] >= midb, 1.0, 0.0)
            return cnt

        cnt = lax.fori_loop(0, nkc, body, jnp.zeros((TQ, LANES), F32))
        return jnp.sum(cnt, axis=1, keepdims=True)

    @pl.when(qi * TQ + TQ <= TOPK)
    def _():
        thr_ref[...] = jnp.full((TQ, LANES), jnp.finfo(F32).min, F32)

    @pl.when(qi * TQ + TQ > TOPK)
    def _():
        kf = float(TOPK)

        def cond(st):
            it, _, _, active = st
            return jnp.logical_and(it < BISECT_MAX_IT, active > 0.0)

        def body(st):
            it, lo, hi, _ = st
            mid = lo + (hi - lo) * 0.5
            cnt = count_ge(mid)
            ge = cnt >= kf
            eq = cnt == kf
            lo2 = jnp.where(ge, mid, lo)
            hi2 = jnp.where(eq, mid, jnp.where(ge, hi, mid))
            stuck = jnp.logical_or(mid <= lo, mid >= hi)
            act = jnp.logical_and(lo2 < hi2, jnp.logical_not(stuck))
            return it + 1, lo2, hi2, jnp.max(jnp.where(act, 1.0, 0.0))

        _, lo, _, _ = lax.while_loop(cond, body, (jnp.int32(0), lo0, hi0, jnp.float32(1.0)))
        thr_ref[...] = jnp.broadcast_to(lo, (TQ, LANES))

    m_ref[...] = jnp.full(m_ref.shape, MASKED, F32)
    l_ref[...] = jnp.zeros(l_ref.shape, F32)
    acc_ref[...] = jnp.zeros(acc_ref.shape, F32)
    thr = thr_ref[:, 0:1]

    def attn_body(c, carry):
        off = pl.multiple_of(c * TK, TK)
        s = s_ref[:, pl.ds(off, TK)]
        bias = jnp.where(s >= thr, 0.0, MASKED)
        bias4 = jnp.concatenate([bias] * 4, axis=0)
        kc = kp_ref[pl.ds(off, TK), :]
        vc = v_ref[pl.ds(off, TK), :]
        for g in range(N_KV):
            qg = qs_ref[g * 2 * TQ:(g + 1) * 2 * TQ, :]
            l_even = _dot_nt(qg, kc[:, g * 256:g * 256 + LANES])
            l_odd = _dot_nt(qg, kc[:, g * 256 + LANES:(g + 1) * 256])
            lg = jnp.concatenate([l_even, l_odd], axis=0) + bias4
            m_old = m_ref[g]
            m_new = jnp.maximum(m_old, jnp.max(lg, axis=1, keepdims=True))
            alpha = jnp.exp(m_old - m_new)
            p = jnp.exp(lg - m_new)
            l_ref[g] = alpha * l_ref[g] + jnp.sum(p, axis=1, keepdims=True)
            vp = vc[:, LANES * (g // 2):LANES * (g // 2 + 1)]
            acc_ref[g] = alpha * acc_ref[g] + _dot(p.astype(BF16), vp)
            m_ref[g] = m_new
        return carry

    lax.fori_loop(0, nkc, attn_body, 0)

    lane = lax.broadcasted_iota(jnp.int32, (TQ, LANES), 1)
    for g in range(N_KV):
        o = acc_ref[g] / l_ref[g]
        for pr in range(2):
            xa = o[pr * TQ:(pr + 1) * TQ]
            xb = o[(2 + pr) * TQ:(3 + pr) * TQ]
            if g % 2 == 0:
                left, right = xa, pltpu.roll(xb, 64, axis=1)
            else:
                left, right = pltpu.roll(xa, 64, axis=1), xb
            col = (2 * g + pr) * LANES
            o_ref[:, col:col + LANES] = jnp.where(lane < 64, left, right).astype(o_ref.dtype)


def _attention(a, iw):
    nq = S // TQ
    rowmap = lambda b, i: b * nq + i
    return pl.pallas_call(
        _attn_kernel,
        grid=(B, nq),
        in_specs=[pl.BlockSpec((TQ, ATTN_W), lambda b, i: (rowmap(b, i), 0)),
                  pl.BlockSpec((TQ, ATTN_W), lambda b, i: (rowmap(b, i), 1)),
                  pl.BlockSpec((TQ, LANES), lambda b, i: (rowmap(b, i), 0)),
                  pl.BlockSpec((S, 256), lambda b, i: (b, 13)),
                  pl.BlockSpec((S, 1024), lambda b, i: (b, 2)),
                  pl.BlockSpec((S, 256), lambda b, i: (b, 12))],
        out_specs=pl.BlockSpec((TQ, ATTN_W), lambda b, i: (rowmap(b, i), 0)),
        out_shape=jax.ShapeDtypeStruct((N, ATTN_W), BF16),
        scratch_shapes=[pltpu.VMEM((TQ, S), F32),
                        pltpu.VMEM((TQ, LANES), F32),
                        pltpu.VMEM((8 * TQ, LANES), BF16),
                        pltpu.VMEM((8 * TQ, LANES), BF16),
                        pltpu.VMEM((N_KV, 4 * TQ, 1), F32),
                        pltpu.VMEM((N_KV, 4 * TQ, 1), F32),
                        pltpu.VMEM((N_KV, 4 * TQ, LANES), F32)],
        compiler_params=_params("arbitrary", "arbitrary"),
        name="sparse_attn",
    )(a, a, iw, a, a, a)


CONV_TS = 512


def _conv_kernel(bg_ref, cg_ref, xc_ref, cw_ref, o_ref, carry_ref):
    i = pl.program_id(0)

    @pl.when(i % (S // CONV_TS) == 0)
    def _():
        carry_ref[...] = jnp.zeros(carry_ref.shape, F32)

    u = cg_ref[...] * xc_ref[...]
    prev = carry_ref[...]
    row = lax.broadcasted_iota(jnp.int32, u.shape, 0)
    u1 = jnp.where(row == 0, prev[7:8, :], pltpu.roll(u, 1, axis=0))
    u2 = jnp.where(row == 0, prev[6:7, :], jnp.where(row == 1, prev[7:8, :], pltpu.roll(u, 2, axis=0)))
    y = cw_ref[0:1, :] * u2 + cw_ref[1:2, :] * u1 + cw_ref[2:3, :] * u
    o_ref[...] = (bg_ref[...] * y).astype(o_ref.dtype)
    carry_ref[...] = u[CONV_TS - 8:CONV_TS, :]


def _short_conv(pb, conv_w):
    return pl.pallas_call(
        _conv_kernel,
        grid=(N // CONV_TS,),
        in_specs=[pl.BlockSpec((CONV_TS, CONV_W), lambda i: (i, 0)),
                  pl.BlockSpec((CONV_TS, CONV_W), lambda i: (i, 1)),
                  pl.BlockSpec((CONV_TS, CONV_W), lambda i: (i, 2)),
                  pl.BlockSpec((3, CONV_W), lambda i: (0, 0))],
        out_specs=pl.BlockSpec((CONV_TS, CONV_W), lambda i: (i, 0)),
        out_shape=jax.ShapeDtypeStruct((N, CONV_W), BF16),
        scratch_shapes=[pltpu.VMEM((8, CONV_W), F32)],
        compiler_params=_params("arbitrary"),
        name="short_conv",
    )(pb, pb, pb, conv_w)


MG_TM = 1024
MG_TN = 512


def _merge_kernel(oa_ref, zb_ref, wa_ref, wb_ref, ga_ref, gb_ref, o_ref):
    ya = _dot(oa_ref[...], wa_ref[...])
    yb = _dot(zb_ref[...], wb_ref[...])
    merged = jax.nn.sigmoid(ga_ref[...]) * ya + jax.nn.sigmoid(gb_ref[...]) * yb
    o_ref[...] = merged.astype(o_ref.dtype)


def _merge(oa, zb, wa, wb, pb):
    ga0 = (3 * CONV_W) // MG_TN
    gb0 = (3 * CONV_W + D) // MG_TN
    return pl.pallas_call(
        _merge_kernel,
        grid=(N // MG_TM, D // MG_TN),
        in_specs=[pl.BlockSpec((MG_TM, ATTN_W), lambda i, j: (i, 0)),
                  pl.BlockSpec((MG_TM, CONV_W), lambda i, j: (i, 0)),
                  pl.BlockSpec((ATTN_W, MG_TN), lambda i, j: (0, j)),
                  pl.BlockSpec((CONV_W, MG_TN), lambda i, j: (0, j)),
                  pl.BlockSpec((MG_TM, MG_TN), lambda i, j: (i, ga0 + j)),
                  pl.BlockSpec((MG_TM, MG_TN), lambda i, j: (i, gb0 + j))],
        out_specs=pl.BlockSpec((MG_TM, MG_TN), lambda i, j: (i, j)),
        out_shape=jax.ShapeDtypeStruct((N, D), BF16),
        compiler_params=_params("arbitrary", "arbitrary"),
        name="gated_merge",
    )(oa, zb, wa, wb, pb, pb)


def _wo_kernel(m_ref, w_ref, x_ref, mod_ref, o_ref):
    o_ref[...] = x_ref[...] + mod_ref[2:3, :] * _dot(m_ref[...], w_ref[...])


def _out_proj(merged, wo, x2, mod3):
    tpb = S // MG_TM
    return pl.pallas_call(
        _wo_kernel,
        grid=(N // MG_TM, D // MG_TN),
        in_specs=[pl.BlockSpec((MG_TM, D), lambda i, j: (i, 0)),
                  pl.BlockSpec((D, MG_TN), lambda i, j: (0, j)),
                  pl.BlockSpec((MG_TM, MG_TN), lambda i, j: (i, j)),
                  pl.BlockSpec((None, N_MOD, MG_TN), lambda i, j: (i // tpb, 0, j))],
        out_specs=pl.BlockSpec((MG_TM, MG_TN), lambda i, j: (i, j)),
        out_shape=jax.ShapeDtypeStruct((N, D), F32),
        compiler_params=_params("arbitrary", "arbitrary"),
        name="out_proj",
    )(merged, wo, x2, mod3)


RT_TM = 256


def _router_kernel(x_ref, g_ref, mod_ref, wr_ref, br_ref,
                   h_ref, mi_ref, mw_ref, cnt_ref, carry_ref):
    i = pl.program_id(0)

    @pl.when(i == 0)
    def _():
        carry_ref[...] = jnp.zeros(carry_ref.shape, F32)

    h = _modulated_norm(x_ref[...], g_ref[...], mod_ref[3:4, :], mod_ref[4:5, :])
    h_ref[...] = h
    lg = _dot(h.astype(BF16), wr_ref[...]) + br_ref[...]
    lane = lax.broadcasted_iota(jnp.int32, lg.shape, 1)
    ninf = -jnp.inf

    gl = jnp.where(lane < N_GROUPS, lg, ninf)
    gm = jnp.max(gl, axis=1, keepdims=True)
    g_sel = jnp.min(jnp.where(gl == gm, lane, LANES), axis=1, keepdims=True)
    p_g = 1.0 / jnp.sum(jnp.exp(gl - gm), axis=1, keepdims=True)

    grp = jnp.right_shift(lane - N_GROUPS, 3)
    in_grp = jnp.logical_and(jnp.logical_and(lane >= N_GROUPS, lane < N_GROUPS + N_EXP), grp == g_sel)
    el = jnp.where(in_grp, lg, ninf)
    m1 = jnp.max(el, axis=1, keepdims=True)
    i1 = jnp.min(jnp.where(el == m1, lane, LANES), axis=1, keepdims=True)
    el2 = jnp.where(lane == i1, ninf, el)
    m2 = jnp.max(el2, axis=1, keepdims=True)
    i2 = jnp.min(jnp.where(el2 == m2, lane, LANES), axis=1, keepdims=True)
    e = jnp.exp(m2 - m1)
    w1 = p_g * (1.0 / (1.0 + e))
    w2 = p_g * (e / (1.0 + e))
    e1 = i1 - N_GROUPS
    e2 = i2 - N_GROUPS

    oh1 = lane == e1
    oh2 = lane == e2
    used = jnp.where(jnp.logical_or(oh1, oh2), 1.0, 0.0)
    ri = lax.broadcasted_iota(jnp.int32, (RT_TM, RT_TM), 0)
    ci = lax.broadcasted_iota(jnp.int32, (RT_TM, RT_TM), 1)
    lower = jnp.where(ci < ri, 1.0, 0.0).astype(BF16)
    before = _dot(lower, used.astype(BF16)) + carry_ref[...]
    r1 = jnp.sum(jnp.where(oh1, before, 0.0), axis=1, keepdims=True).astype(jnp.int32)
    r2 = jnp.sum(jnp.where(oh2, before, 0.0), axis=1, keepdims=True).astype(jnp.int32)
    total = carry_ref[...] + jnp.sum(used, axis=0, keepdims=True)
    carry_ref[...] = total
    cnt_ref[...] = total

    zero = jnp.zeros_like(lane)
    mi_ref[...] = jnp.where(lane == 0, e1, jnp.where(lane == 1, e2,
                            jnp.where(lane == 2, r1, jnp.where(lane == 3, r2, zero))))
    mw_ref[...] = jnp.where(lane == 0, w1, jnp.where(lane == 1, w2, 0.0))


def _router(x1, g, mod3, wr, br):
    tpb = S // RT_TM
    return pl.pallas_call(
        _router_kernel,
        grid=(N // RT_TM,),
        in_specs=[pl.BlockSpec((RT_TM, D), lambda i: (i, 0)),
                  pl.BlockSpec((1, D), lambda i: (0, 0)),
                  pl.BlockSpec((None, N_MOD, D), lambda i: (i // tpb, 0, 0)),
                  pl.BlockSpec((D, LANES), lambda i: (0, 0)),
                  pl.BlockSpec((1, LANES), lambda i: (0, 0))],
        out_specs=[pl.BlockSpec((RT_TM, D), lambda i: (i, 0)),
                   pl.BlockSpec((RT_TM, LANES), lambda i: (i, 0)),
                   pl.BlockSpec((RT_TM, LANES), lambda i: (i, 0)),
                   pl.BlockSpec((1, LANES), lambda i: (0, 0))],
        out_shape=[jax.ShapeDtypeStruct((N, D), F32),
                   jax.ShapeDtypeStruct((N, LANES), jnp.int32),
                   jax.ShapeDtypeStruct((N, LANES), F32),
                   jax.ShapeDtypeStruct((1, LANES), F32)],
        scratch_shapes=[pltpu.VMEM((1, LANES), F32)],
        compiler_params=_params("arbitrary"),
        name="norm2_router",
    )(x1, g.reshape(1, D), mod3, wr, br)


DP_TM = 256


def _row_copy(src_ref, src_row, dst_ref, dst_row, sem):
    return pltpu.make_async_copy(src_ref.at[pl.ds(src_row, 1), :], dst_ref.at[pl.ds(dst_row, 1), :], sem)


def _dispatch_kernel(eid_ref, rank_ref, offs_ref, h_ref, xs_in_ref, xs_ref, sem):
    del xs_in_ref
    base = pl.program_id(0) * DP_TM

    def issue(r, carry):
        for k in range(2):
            p = (base + r) * 2 + k
            slot = offs_ref[eid_ref[p]] + rank_ref[p]
            _row_copy(h_ref, r, xs_ref, slot, sem).start()
        return carry

    lax.fori_loop(0, DP_TM, issue, 0)

    def drain(r, carry):
        for _ in range(2):
            _row_copy(h_ref, 0, xs_ref, 0, sem).wait()
        return carry

    lax.fori_loop(0, DP_TM, drain, 0)


def _dispatch(eid, rank, offs, h2, xs0):
    grid_spec = pltpu.PrefetchScalarGridSpec(
        num_scalar_prefetch=3,
        grid=(N // DP_TM,),
        in_specs=[pl.BlockSpec((DP_TM, D), lambda i, *_: (i, 0)),
                  pl.BlockSpec(memory_space=pl.ANY)],
        out_specs=pl.BlockSpec(memory_space=pl.ANY),
        scratch_shapes=[pltpu.SemaphoreType.DMA(())],
    )
    return pl.pallas_call(
        _dispatch_kernel,
        grid_spec=grid_spec,
        out_shape=jax.ShapeDtypeStruct((XS_ROWS, D), F32),
        input_output_aliases={4: 0},
        compiler_params=_params("arbitrary"),
        name="moe_dispatch",
    )(eid, rank, offs, h2, xs0)


def _expert_kernel(ce_ref, nc_ref, x_ref, wg_ref, wu_ref, wd_ref, y_ref, wg_s, wu_s, wd_s):
    c = pl.program_id(0)
    valid = c < nc_ref[0]
    new_expert = jnp.logical_or(c == 0, ce_ref[c] != ce_ref[jnp.maximum(c - 1, 0)])

    @pl.when(jnp.logical_and(valid, new_expert))
    def _():
        wg_s[...] = wg_ref[...].astype(BF16)
        wu_s[...] = wu_ref[...].astype(BF16)
        wd_s[...] = wd_ref[...].astype(BF16)

    @pl.when(valid)
    def _():
        x = x_ref[...].astype(BF16)
        a = _dot(x, wg_s[...])
        u = _dot(x, wu_s[...])
        act = (a * jax.nn.sigmoid(a)) * u
        y_ref[...] = _dot(act.astype(BF16), wd_s[...])


def _experts(ce, nc, xs, wg, wu, wd):
    rows = lambda c, ce, nc: (jnp.minimum(c, nc[0] - 1), 0)
    wmap = lambda c, ce, nc: (ce[c], 0, 0)
    grid_spec = pltpu.PrefetchScalarGridSpec(
        num_scalar_prefetch=2,
        grid=(MAX_CHUNKS,),
        in_specs=[pl.BlockSpec((CH, D), rows),
                  pl.BlockSpec((None, D, FF), wmap),
                  pl.BlockSpec((None, D, FF), wmap),
                  pl.BlockSpec((None, FF, D), wmap)],
        out_specs=pl.BlockSpec((CH, D), rows),
        scratch_shapes=[pltpu.VMEM((D, FF), BF16), pltpu.VMEM((D, FF), BF16), pltpu.VMEM((FF, D), BF16)],
    )
    return pl.pallas_call(
        _expert_kernel,
        grid_spec=grid_spec,
        out_shape=jax.ShapeDtypeStruct((XS_ROWS, D), F32),
        input_output_aliases={2: 0},
        compiler_params=_params("arbitrary"),
        name="moe_experts",
    )(ce, nc, xs, wg, wu, wd)


CB_TM = 256


def _combine_kernel(eid_ref, rank_ref, offs_ref, y_ref, x_ref, mw_ref, mod_ref, g_ref, o_ref, ybuf, sem):
    base = pl.program_id(0) * CB_TM

    def issue(r, carry):
        for k in range(2):
            p = (base + r) * 2 + k
            slot = offs_ref[eid_ref[p]] + rank_ref[p]
            _row_copy(y_ref, slot, ybuf.at[k], r, sem).start()
        return carry

    lax.fori_loop(0, CB_TM, issue, 0)

    def drain(r, carry):
        for k in range(2):
            _row_copy(y_ref, 0, ybuf.at[k], 0, sem).wait()
        return carry

    lax.fori_loop(0, CB_TM, drain, 0)

    moe = mw_ref[:, 0:1] * ybuf[0] + mw_ref[:, 1:2] * ybuf[1]
    x = x_ref[...] + mod_ref[5:6, :] * moe
    r = lax.rsqrt(jnp.mean(x * x, axis=-1, keepdims=True) + EPS)
    o_ref[...] = (x * r) * g_ref[...]


def _combine(eid, rank, offs, y, x1, mw, mod3, g):
    tpb = S // CB_TM
    grid_spec = pltpu.PrefetchScalarGridSpec(
        num_scalar_prefetch=3,
        grid=(N // CB_TM,),
        in_specs=[pl.BlockSpec(memory_space=pl.ANY),
                  pl.BlockSpec((CB_TM, D), lambda i, *_: (i, 0)),
                  pl.BlockSpec((CB_TM, LANES), lambda i, *_: (i, 0)),
                  pl.BlockSpec((None, N_MOD, D), lambda i, *_: (i // tpb, 0, 0)),
                  pl.BlockSpec((1, D), lambda i, *_: (0, 0))],
        out_specs=pl.BlockSpec((CB_TM, D), lambda i, *_: (i, 0)),
        scratch_shapes=[pltpu.VMEM((2, CB_TM, D), F32), pltpu.SemaphoreType.DMA(())],
    )
    return pl.pallas_call(
        _combine_kernel,
        grid_spec=grid_spec,
        out_shape=jax.ShapeDtypeStruct((N, D), F32),
        compiler_params=_params("arbitrary"),
        name="moe_combine_norm",
    )(eid, rank, offs, y, x1, mw, mod3, g.reshape(1, D))


def _pad_heads(w):
    g = w.shape[1] // HEAD_DIM
    w3 = w.reshape(D, g, HEAD_DIM)
    z = jnp.zeros_like(w3)
    return jnp.concatenate([w3, z, z, w3], axis=2).reshape(D, g * 4 * HEAD_DIM)


def kernel(x, c, w_ada, b_ada, g_norm_mix, w_in, conv_w, w_a_up, w_b_out, w_o, g_norm_ffn,
           w_rg, b_rg, w_re, b_re, w_e_gate, w_e_up, w_e_down, g_norm_final):
    x2 = x.reshape(N, D)
    mod3 = _ada(c, w_ada[0], b_ada[0]).reshape(B, N_MOD, D)

    wi = w_in[0]
    o_q, o_k, o_v, o_iq, o_ik, o_iw, o_bg = 0, 1024, 1280, 1536, 2560, 2624, 2640
    w_a = jnp.concatenate([wi[:, o_q:o_k], wi[:, o_iq:o_ik], _pad_heads(wi[:, o_k:o_v]),
                           wi[:, o_v:o_iq], _pad_heads(wi[:, o_ik:o_iw])], axis=1).astype(BF16)
    w_iw = jnp.pad(wi[:, o_iw:o_bg], ((0, 0), (0, LANES - IDX_HEADS))).astype(BF16)
    w_b = wi[:, o_bg:].astype(BF16)

    h = _norm1(x2, g_norm_mix[0], mod3)
    pa = _matmul(h, w_a, BF16, 1024, 512, "proj_attn")
    iw = _matmul(h, w_iw, F32, 1024, LANES, "proj_iw")
    pb = _matmul(h, w_b, F32, 1024, 512, "proj_conv_gates")

    oa = _attention(pa, iw)
    zb = _short_conv(pb, conv_w[0])
    merged = _merge(oa, zb, w_a_up[0].astype(BF16), w_b_out[0].astype(BF16), pb)
    x1 = _out_proj(merged, w_o[0].astype(BF16), x2, mod3)

    wr = jnp.pad(jnp.concatenate([w_rg[0], w_re[0]], axis=1),
                 ((0, 0), (0, LANES - N_GROUPS - N_EXP))).astype(BF16)
    br = jnp.pad(jnp.concatenate([b_rg[0], b_re[0]]), (0, LANES - N_GROUPS - N_EXP)).reshape(1, LANES)
    h2, mi, mw, cnt = _router(x1, g_norm_ffn[0], mod3, wr, br)

    counts = cnt[0, :N_EXP].astype(jnp.int32)
    nch = (counts + CH - 1) // CH
    cum = jnp.cumsum(nch)
    offs = ((cum - nch) * CH).astype(jnp.int32)
    total = cum[-1]
    cidx = jnp.minimum(jnp.arange(MAX_CHUNKS, dtype=jnp.int32), total - 1)
    ce = jnp.minimum(jnp.searchsorted(cum, cidx, side="right"), N_EXP - 1).astype(jnp.int32)
    nc = total.reshape(1).astype(jnp.int32)
    eid = mi[:, 0:2].reshape(-1)
    rank = mi[:, 2:4].reshape(-1)

    xs = _dispatch(eid, rank, offs, h2, jnp.zeros((XS_ROWS, D), F32))
    y = _experts(ce, nc, xs, w_e_gate[0], w_e_up[0], w_e_down[0])
    out = _combine(eid, rank, offs, y, x1, mw, mod3, g_norm_final)
    return out.reshape(B, S, D)
```

```python
import functools

import jax
import jax.numpy as jnp
from jax import lax
from jax.experimental import pallas as pl
from jax.experimental.pallas import tpu as pltpu

F32 = jnp.float32
BF16 = jnp.bfloat16

D = 2048
B = 2
S = 4096
N = B * S
EPS = 1e-6
HEAD_DIM = 64
ATTN_W = 1024
N_KV = 4
KV_W = 256
IDX_HEADS = 16
IDX_DIM = 64
TOPK = min(256, S // 4)
CONV_W = 1024
N_GROUPS = 4
EPG = 8
N_EXP = 32
FF = 512
N_MOD = 6

LANES = 128
TQ = 128
TK = 512
CH = 256
MAX_CHUNKS = (N * 2) // CH + N_EXP
XS_ROWS = MAX_CHUNKS * CH
MASKED = -1e30
BISECT_MAX_IT = 64
VMEM_LIMIT = 56 * 1024 * 1024


def _dot(a, b):
    return jnp.dot(a, b, preferred_element_type=F32)


def _dot_nt(a, b):
    return lax.dot_general(a, b, (((1,), (1,)), ((), ())), preferred_element_type=F32)


def _params(*sem):
    return pltpu.CompilerParams(dimension_semantics=sem, vmem_limit_bytes=VMEM_LIMIT)


ADA_TN = 1024
ADA_KC = 256


def _ada_kernel(ct_ref, w_ref, b_ref, o_ref):
    tn = w_ref.shape[1]

    def body(i, accs):
        k0 = pl.multiple_of(i * ADA_KC, ADA_KC)
        ct = ct_ref[pl.ds(k0, ADA_KC), :]
        act = ct * jax.nn.sigmoid(ct)
        w = w_ref[pl.ds(k0, ADA_KC), :]
        out = []
        for b in range(B):
            prod = w * act[:, b:b + 1]
            out.append(accs[b] + prod.reshape(ADA_KC // 8, 8, tn).sum(axis=0))
        return tuple(out)

    accs = lax.fori_loop(0, D // ADA_KC, body, tuple(jnp.zeros((8, tn), F32) for _ in range(B)))
    rows = [jnp.sum(a, axis=0, keepdims=True) for a in accs]
    o_ref[...] = jnp.concatenate(rows, axis=0) + b_ref[...]


def _ada(c, w, b):
    n_out = w.shape[1]
    return pl.pallas_call(
        _ada_kernel,
        grid=(n_out // ADA_TN,),
        in_specs=[pl.BlockSpec((D, B), lambda j: (0, 0)),
                  pl.BlockSpec((D, ADA_TN), lambda j: (0, j)),
                  pl.BlockSpec((1, ADA_TN), lambda j: (0, j))],
        out_specs=pl.BlockSpec((B, ADA_TN), lambda j: (0, j)),
        out_shape=jax.ShapeDtypeStruct((B, n_out), F32),
        compiler_params=_params("arbitrary"),
        name="ada_mod",
    )(c.T, w, b.reshape(1, n_out))


NORM_TM = 512


def _modulated_norm(x, g, shift, scale):
    r = lax.rsqrt(jnp.mean(x * x, axis=-1, keepdims=True) + EPS)
    return (x * r) * g * (1.0 + scale) + shift


def _norm1_kernel(x_ref, g_ref, mod_ref, h_ref):
    h = _modulated_norm(x_ref[...], g_ref[...], mod_ref[0:1, :], mod_ref[1:2, :])
    h_ref[...] = h.astype(h_ref.dtype)


def _norm1(x2, g, mod3):
    tpb = S // NORM_TM
    return pl.pallas_call(
        _norm1_kernel,
        grid=(N // NORM_TM,),
        in_specs=[pl.BlockSpec((NORM_TM, D), lambda i: (i, 0)),
                  pl.BlockSpec((1, D), lambda i: (0, 0)),
                  pl.BlockSpec((None, N_MOD, D), lambda i: (i // tpb, 0, 0))],
        out_specs=pl.BlockSpec((NORM_TM, D), lambda i: (i, 0)),
        out_shape=jax.ShapeDtypeStruct((N, D), BF16),
        compiler_params=_params("arbitrary"),
        name="norm1",
    )(x2, g.reshape(1, D), mod3)


def _mm_kernel(a_ref, w_ref, o_ref):
    o_ref[...] = _dot(a_ref[...], w_ref[...]).astype(o_ref.dtype)


def _matmul(a, w, out_dtype, tm, tn, name):
    m, k = a.shape
    n = w.shape[1]
    return pl.pallas_call(
        _mm_kernel,
        grid=(m // tm, n // tn),
        in_specs=[pl.BlockSpec((tm, k), lambda i, j: (i, 0)),
                  pl.BlockSpec((k, tn), lambda i, j: (0, j))],
        out_specs=pl.BlockSpec((tm, tn), lambda i, j: (i, j)),
        out_shape=jax.ShapeDtypeStruct((m, n), out_dtype),
        compiler_params=_params("arbitrary", "arbitrary"),
        name=name,
    )(a, w)


A_COLS = 2 * ATTN_W + 4 * KV_W + KV_W + 4 * IDX_DIM


def _attn_kernel(q_ref, iq_ref, iw_ref, ikp_ref, kp_ref, v_ref, o_ref,
                 s_ref, thr_ref, qs_ref, iqs_ref, m_ref, l_ref, acc_ref, a_ref, lg_ref):
    qi = pl.program_id(1)
    nkc = qi // (TK // TQ) + 1
    rowg = qi * TQ + lax.broadcasted_iota(jnp.int32, (TQ, TK), 0)
    coll = lax.broadcasted_iota(jnp.int32, (TQ, TK), 1)

    for j in range(8):
        iqs_ref[j * TQ:(j + 1) * TQ, :] = iq_ref[:, LANES * j:LANES * (j + 1)]
        qs_ref[j * TQ:(j + 1) * TQ, :] = q_ref[:, LANES * j:LANES * (j + 1)] * (HEAD_DIM ** -0.5)

    w = iw_ref[:, 0:IDX_HEADS] * ((IDX_HEADS ** -0.5) * (IDX_DIM ** -0.5))

    def score_body(c, carry):
        lo, hi = carry
        off = pl.multiple_of(c * TK, TK)
        ikc = ikp_ref[pl.ds(off, TK), :]
        iqs = iqs_ref[...]
        d_even = _dot_nt(iqs, ikc[:, 0:LANES])
        d_odd = _dot_nt(iqs, ikc[:, LANES:2 * LANES])
        acc = jnp.zeros((TQ, TK), F32)
        for j in range(8):
            acc = acc + jnp.maximum(d_even[j * TQ:(j + 1) * TQ], 0.0) * w[:, 2 * j:2 * j + 1]
            acc = acc + jnp.maximum(d_odd[j * TQ:(j + 1) * TQ], 0.0) * w[:, 2 * j + 1:2 * j + 2]
        valid = (off + coll) <= rowg
        s_ref[:, pl.ds(off, TK)] = jnp.where(valid, acc, -jnp.inf)
        hi = jnp.maximum(hi, jnp.max(jnp.where(valid, acc, -jnp.inf), axis=1, keepdims=True))
        lo = jnp.minimum(lo, jnp.min(jnp.where(valid, acc, jnp.inf), axis=1, keepdims=True))
        return lo, hi

    lo0, hi0 = lax.fori_loop(0, nkc, score_body,
                             (jnp.full((TQ, 1), jnp.inf, F32), jnp.full((TQ, 1), -jnp.inf, F32)))

    def count_ge(mid):
        midb = jnp.broadcast_to(mid, (TQ, LANES))

        def body(c, cnt):
            off = pl.multiple_of(c * TK, TK)
            s = s_ref[:, pl.ds(off, TK)]
            for i in range(TK // LANES):
                cnt = cnt + jnp.where(s[:, LANES * i:LANES * (i + 1)] >= midb, 1.0, 0.0)
            return cnt

        cnt = lax.fori_loop(0, nkc, body, jnp.zeros((TQ, LANES), F32))
        return jnp.sum(cnt, axis=1, keepdims=True)

    @pl.when(qi * TQ + TQ <= TOPK)
    def _():
        thr_ref[...] = jnp.full((TQ, LANES), jnp.finfo(F32).min, F32)

    @pl.when(qi * TQ + TQ > TOPK)
    def _():
        kf = float(TOPK)

        def cond(st):
            it, _, _, active = st
            return jnp.logical_and(it < BISECT_MAX_IT, active > 0.0)

        def body(st):
            it, lo, hi, _ = st
            mid = lo + (hi - lo) * 0.5
            cnt = count_ge(mid)
            ge = cnt >= kf
            eq = cnt == kf
            lo2 = jnp.where(ge, mid, lo)
            hi2 = jnp.where(eq, mid, jnp.where(ge, hi, mid))
            stuck = jnp.logical_or(mid <= lo, mid >= hi)
            act = jnp.logical_and(lo2 < hi2, jnp.logical_not(stuck))
            return it + 1, lo2, hi2, jnp.max(jnp.where(act, 1.0, 0.0))

        _, lo, _, _ = lax.while_loop(cond, body, (jnp.int32(0), lo0, hi0, jnp.float32(1.0)))
        thr_ref[...] = jnp.broadcast_to(lo, (TQ, LANES))

    m_ref[...] = jnp.full(m_ref.shape, MASKED, F32)
    l_ref[...] = jnp.zeros(l_ref.shape, F32)
    acc_ref[...] = jnp.zeros(acc_ref.shape, F32)
    thr = thr_ref[...]
    nt = TK // LANES
    n_blk = 2 * N_KV

    def attn_body(c, carry):
        off = pl.multiple_of(c * TK, TK)
        s = s_ref[:, pl.ds(off, TK)]
        bias = jnp.concatenate(
            [jnp.where(s[:, LANES * t:LANES * (t + 1)] >= thr, 0.0, MASKED) for t in range(nt)], axis=1)
        bias2 = jnp.concatenate([bias, bias], axis=0)

        for blk in range(n_blk):
            g, par = blk // 2, blk % 2
            qg = qs_ref[g * 2 * TQ:(g + 1) * 2 * TQ, :]
            kk = kp_ref[pl.ds(off, TK), g * 256 + par * LANES:g * 256 + (par + 1) * LANES]
            lg = _dot_nt(qg, kk) + bias2
            lg_ref[blk] = lg
            m_old = m_ref[blk]
            m_new = jnp.maximum(m_old, jnp.max(lg, axis=1, keepdims=True))
            a_ref[blk] = jnp.exp(m_old - m_new)
            m_ref[blk] = m_new

        for blk in range(n_blk):
            g = blk // 2
            m_new = m_ref[blk]
            alpha = a_ref[blk]
            ps = [jnp.exp(lg_ref[blk, :, LANES * t:LANES * (t + 1)] - m_new) for t in range(nt)]
            psum = ps[0]
            for t in range(1, nt):
                psum = psum + ps[t]
            l_ref[blk] = alpha * l_ref[blk] + psum
            p = jnp.concatenate(ps, axis=1).astype(BF16)
            vp = v_ref[pl.ds(off, TK), LANES * (g // 2):LANES * (g // 2 + 1)]
            acc_ref[blk] = alpha * acc_ref[blk] + _dot(p, vp)
        return carry

    lax.fori_loop(0, nkc, attn_body, 0)

    lane = lax.broadcasted_iota(jnp.int32, (TQ, LANES), 1)
    for g in range(N_KV):
        o_even = acc_ref[2 * g] / jnp.sum(l_ref[2 * g], axis=1, keepdims=True)
        o_odd = acc_ref[2 * g + 1] / jnp.sum(l_ref[2 * g + 1], axis=1, keepdims=True)
        for pr in range(2):
            xa = o_even[pr * TQ:(pr + 1) * TQ]
            xb = o_odd[pr * TQ:(pr + 1) * TQ]
            if g % 2 == 0:
                left, right = xa, pltpu.roll(xb, 64, axis=1)
            else:
                left, right = pltpu.roll(xa, 64, axis=1), xb
            col = (2 * g + pr) * LANES
            o_ref[:, col:col + LANES] = jnp.where(lane < 64, left, right).astype(o_ref.dtype)


def _attention(a, iw):
    nq = S // TQ
    rowmap = lambda b, i: b * nq + i
    return pl.pallas_call(
        _attn_kernel,
        grid=(B, nq),
        in_specs=[pl.BlockSpec((TQ, ATTN_W), lambda b, i: (rowmap(b, i), 0)),
                  pl.BlockSpec((TQ, ATTN_W), lambda b, i: (rowmap(b, i), 1)),
                  pl.BlockSpec((TQ, LANES), lambda b, i: (rowmap(b, i), 0)),
                  pl.BlockSpec((S, 256), lambda b, i: (b, 13)),
                  pl.BlockSpec((S, 1024), lambda b, i: (b, 2)),
                  pl.BlockSpec((S, 256), lambda b, i: (b, 12))],
        out_specs=pl.BlockSpec((TQ, ATTN_W), lambda b, i: (rowmap(b, i), 0)),
        out_shape=jax.ShapeDtypeStruct((N, ATTN_W), BF16),
        scratch_shapes=[pltpu.VMEM((TQ, S), F32),
                        pltpu.VMEM((TQ, LANES), F32),
                        pltpu.VMEM((8 * TQ, LANES), BF16),
                        pltpu.VMEM((8 * TQ, LANES), BF16),
                        pltpu.VMEM((2 * N_KV, 2 * TQ, LANES), F32),
                        pltpu.VMEM((2 * N_KV, 2 * TQ, LANES), F32),
                        pltpu.VMEM((2 * N_KV, 2 * TQ, LANES), F32),
                        pltpu.VMEM((2 * N_KV, 2 * TQ, LANES), F32),
                        pltpu.VMEM((2 * N_KV, 2 * TQ, TK), F32)],
        compiler_params=_params("arbitrary", "arbitrary"),
        name="sparse_attn",
    )(a, a, iw, a, a, a)


CONV_TS = 512


def _conv_kernel(bg_ref, cg_ref, xc_ref, cw_ref, o_ref, carry_ref):
    i = pl.program_id(0)

    @pl.when(i % (S // CONV_TS) == 0)
    def _():
        carry_ref[...] = jnp.zeros(carry_ref.shape, F32)

    u = cg_ref[...] * xc_ref[...]
    prev = carry_ref[...]
    row = lax.broadcasted_iota(jnp.int32, u.shape, 0)
    u1 = jnp.where(row == 0, prev[7:8, :], pltpu.roll(u, 1, axis=0))
    u2 = jnp.where(row == 0, prev[6:7, :], jnp.where(row == 1, prev[7:8, :], pltpu.roll(u, 2, axis=0)))
    y = cw_ref[0:1, :] * u2 + cw_ref[1:2, :] * u1 + cw_ref[2:3, :] * u
    o_ref[...] = (bg_ref[...] * y).astype(o_ref.dtype)
    carry_ref[...] = u[CONV_TS - 8:CONV_TS, :]


def _short_conv(pb, conv_w):
    return pl.pallas_call(
        _conv_kernel,
        grid=(N // CONV_TS,),
        in_specs=[pl.BlockSpec((CONV_TS, CONV_W), lambda i: (i, 0)),
                  pl.BlockSpec((CONV_TS, CONV_W), lambda i: (i, 1)),
                  pl.BlockSpec((CONV_TS, CONV_W), lambda i: (i, 2)),
                  pl.BlockSpec((3, CONV_W), lambda i: (0, 0))],
        out_specs=pl.BlockSpec((CONV_TS, CONV_W), lambda i: (i, 0)),
        out_shape=jax.ShapeDtypeStruct((N, CONV_W), BF16),
        scratch_shapes=[pltpu.VMEM((8, CONV_W), F32)],
        compiler_params=_params("arbitrary"),
        name="short_conv",
    )(pb, pb, pb, conv_w)


MG_TM = 1024
MG_TN = 512


def _merge_kernel(oa_ref, zb_ref, wa_ref, wb_ref, ga_ref, gb_ref, o_ref):
    ya = _dot(oa_ref[...], wa_ref[...])
    yb = _dot(zb_ref[...], wb_ref[...])
    merged = jax.nn.sigmoid(ga_ref[...]) * ya + jax.nn.sigmoid(gb_ref[...]) * yb
    o_ref[...] = merged.astype(o_ref.dtype)


def _merge(oa, zb, wa, wb, pb):
    ga0 = (3 * CONV_W) // MG_TN
    gb0 = (3 * CONV_W + D) // MG_TN
    return pl.pallas_call(
        _merge_kernel,
        grid=(N // MG_TM, D // MG_TN),
        in_specs=[pl.BlockSpec((MG_TM, ATTN_W), lambda i, j: (i, 0)),
                  pl.BlockSpec((MG_TM, CONV_W), lambda i, j: (i, 0)),
                  pl.BlockSpec((ATTN_W, MG_TN), lambda i, j: (0, j)),
                  pl.BlockSpec((CONV_W, MG_TN), lambda i, j: (0, j)),
                  pl.BlockSpec((MG_TM, MG_TN), lambda i, j: (i, ga0 + j)),
                  pl.BlockSpec((MG_TM, MG_TN), lambda i, j: (i, gb0 + j))],
        out_specs=pl.BlockSpec((MG_TM, MG_TN), lambda i, j: (i, j)),
        out_shape=jax.ShapeDtypeStruct((N, D), BF16),
        compiler_params=_params("arbitrary", "arbitrary"),
        name="gated_merge",
    )(oa, zb, wa, wb, pb, pb)


def _wo_kernel(m_ref, w_ref, x_ref, mod_ref, o_ref):
    o_ref[...] = x_ref[...] + mod_ref[2:3, :] * _dot(m_ref[...], w_ref[...])


def _out_proj(merged, wo, x2, mod3):
    tpb = S // MG_TM
    return pl.pallas_call(
        _wo_kernel,
        grid=(N // MG_TM, D // MG_TN),
        in_specs=[pl.BlockSpec((MG_TM, D), lambda i, j: (i, 0)),
                  pl.BlockSpec((D, MG_TN), lambda i, j: (0, j)),
                  pl.BlockSpec((MG_TM, MG_TN), lambda i, j: (i, j)),
                  pl.BlockSpec((None, N_MOD, MG_TN), lambda i, j: (i // tpb, 0, j))],
        out_specs=pl.BlockSpec((MG_TM, MG_TN), lambda i, j: (i, j)),
        out_shape=jax.ShapeDtypeStruct((N, D), F32),
        compiler_params=_params("arbitrary", "arbitrary"),
        name="out_proj",
    )(merged, wo, x2, mod3)


RT_TM = 256


def _router_kernel(x_ref, g_ref, mod_ref, wr_ref, br_ref,
                   h_ref, mi_ref, mw_ref, cnt_ref, carry_ref):
    i = pl.program_id(0)

    @pl.when(i == 0)
    def _():
        carry_ref[...] = jnp.zeros(carry_ref.shape, F32)

    h = _modulated_norm(x_ref[...], g_ref[...], mod_ref[3:4, :], mod_ref[4:5, :])
    h_ref[...] = h
    lg = _dot(h.astype(BF16), wr_ref[...]) + br_ref[...]
    lane = lax.broadcasted_iota(jnp.int32, lg.shape, 1)
    ninf = -jnp.inf

    gl = jnp.where(lane < N_GROUPS, lg, ninf)
    gm = jnp.max(gl, axis=1, keepdims=True)
    g_sel = jnp.min(jnp.where(gl == gm, lane, LANES), axis=1, keepdims=True)
    p_g = 1.0 / jnp.sum(jnp.exp(gl - gm), axis=1, keepdims=True)

    grp = jnp.right_shift(lane - N_GROUPS, 3)
    in_grp = jnp.logical_and(jnp.logical_and(lane >= N_GROUPS, lane < N_GROUPS + N_EXP), grp == g_sel)
    el = jnp.where(in_grp, lg, ninf)
    m1 = jnp.max(el, axis=1, keepdims=True)
    i1 = jnp.min(jnp.where(el == m1, lane, LANES), axis=1, keepdims=True)
    el2 = jnp.where(lane == i1, ninf, el)
    m2 = jnp.max(el2, axis=1, keepdims=True)
    i2 = jnp.min(jnp.where(el2 == m2, lane, LANES), axis=1, keepdims=True)
    e = jnp.exp(m2 - m1)
    w1 = p_g * (1.0 / (1.0 + e))
    w2 = p_g * (e / (1.0 + e))
    e1 = i1 - N_GROUPS
    e2 = i2 - N_GROUPS

    oh1 = lane == e1
    oh2 = lane == e2
    used = jnp.where(jnp.logical_or(oh1, oh2), 1.0, 0.0)
    ri = lax.broadcasted_iota(jnp.int32, (RT_TM, RT_TM), 0)
    ci = lax.broadcasted_iota(jnp.int32, (RT_TM, RT_TM), 1)
    lower = jnp.where(ci < ri, 1.0, 0.0).astype(BF16)
    before = _dot(lower, used.astype(BF16)) + carry_ref[...]
    r1 = jnp.sum(jnp.where(oh1, before, 0.0), axis=1, keepdims=True).astype(jnp.int32)
    r2 = jnp.sum(jnp.where(oh2, before, 0.0), axis=1, keepdims=True).astype(jnp.int32)
    total = carry_ref[...] + jnp.sum(used, axis=0, keepdims=True)
    carry_ref[...] = total
    cnt_ref[...] = total

    zero = jnp.zeros_like(lane)
    mi_ref[...] = jnp.where(lane == 0, e1, jnp.where(lane == 1, e2,
                            jnp.where(lane == 2, r1, jnp.where(lane == 3, r2, zero))))
    mw_ref[...] = jnp.where(lane == 0, w1, jnp.where(lane == 1, w2, 0.0))


def _router(x1, g, mod3, wr, br):
    tpb = S // RT_TM
    return pl.pallas_call(
        _router_kernel,
        grid=(N // RT_TM,),
        in_specs=[pl.BlockSpec((RT_TM, D), lambda i: (i, 0)),
                  pl.BlockSpec((1, D), lambda i: (0, 0)),
                  pl.BlockSpec((None, N_MOD, D), lambda i: (i // tpb, 0, 0)),
                  pl.BlockSpec((D, LANES), lambda i: (0, 0)),
                  pl.BlockSpec((1, LANES), lambda i: (0, 0))],
        out_specs=[pl.BlockSpec((RT_TM, D), lambda i: (i, 0)),
                   pl.BlockSpec((RT_TM, LANES), lambda i: (i, 0)),
                   pl.BlockSpec((RT_TM, LANES), lambda i: (i, 0)),
                   pl.BlockSpec((1, LANES), lambda i: (0, 0))],
        out_shape=[jax.ShapeDtypeStruct((N, D), F32),
                   jax.ShapeDtypeStruct((N, LANES), jnp.int32),
                   jax.ShapeDtypeStruct((N, LANES), F32),
                   jax.ShapeDtypeStruct((1, LANES), F32)],
        scratch_shapes=[pltpu.VMEM((1, LANES), F32)],
        compiler_params=_params("arbitrary"),
        name="norm2_router",
    )(x1, g.reshape(1, D), mod3, wr, br)


DP_TM = 256


def _row_copy(src_ref, src_row, dst_ref, dst_row, sem):
    return pltpu.make_async_copy(src_ref.at[pl.ds(src_row, 1), :], dst_ref.at[pl.ds(dst_row, 1), :], sem)


def _dispatch_kernel(eid_ref, rank_ref, offs_ref, h_ref, xs_in_ref, xs_ref, sem):
    del xs_in_ref
    base = pl.program_id(0) * DP_TM

    def issue(r, carry):
        for k in range(2):
            p = (base + r) * 2 + k
            slot = offs_ref[eid_ref[p]] + rank_ref[p]
            _row_copy(h_ref, r, xs_ref, slot, sem).start()
        return carry

    lax.fori_loop(0, DP_TM, issue, 0)

    def drain(r, carry):
        for _ in range(2):
            _row_copy(h_ref, 0, xs_ref, 0, sem).wait()
        return carry

    lax.fori_loop(0, DP_TM, drain, 0)


def _dispatch(eid, rank, offs, h2, xs0):
    grid_spec = pltpu.PrefetchScalarGridSpec(
        num_scalar_prefetch=3,
        grid=(N // DP_TM,),
        in_specs=[pl.BlockSpec((DP_TM, D), lambda i, *_: (i, 0)),
                  pl.BlockSpec(memory_space=pl.ANY)],
        out_specs=pl.BlockSpec(memory_space=pl.ANY),
        scratch_shapes=[pltpu.SemaphoreType.DMA(())],
    )
    return pl.pallas_call(
        _dispatch_kernel,
        grid_spec=grid_spec,
        out_shape=jax.ShapeDtypeStruct((XS_ROWS, D), F32),
        input_output_aliases={4: 0},
        compiler_params=_params("arbitrary"),
        name="moe_dispatch",
    )(eid, rank, offs, h2, xs0)


def _expert_kernel(ce_ref, nc_ref, x_ref, wg_ref, wu_ref, wd_ref, y_ref, wg_s, wu_s, wd_s):
    c = pl.program_id(0)
    valid = c < nc_ref[0]
    new_expert = jnp.logical_or(c == 0, ce_ref[c] != ce_ref[jnp.maximum(c - 1, 0)])

    @pl.when(jnp.logical_and(valid, new_expert))
    def _():
        wg_s[...] = wg_ref[...].astype(BF16)
        wu_s[...] = wu_ref[...].astype(BF16)
        wd_s[...] = wd_ref[...].astype(BF16)

    @pl.when(valid)
    def _():
        x = x_ref[...].astype(BF16)
        a = _dot(x, wg_s[...])
        u = _dot(x, wu_s[...])
        act = (a * jax.nn.sigmoid(a)) * u
        y_ref[...] = _dot(act.astype(BF16), wd_s[...])


def _experts(ce, nc, xs, wg, wu, wd):
    rows = lambda c, ce, nc: (jnp.minimum(c, nc[0] - 1), 0)
    wmap = lambda c, ce, nc: (ce[c], 0, 0)
    grid_spec = pltpu.PrefetchScalarGridSpec(
        num_scalar_prefetch=2,
        grid=(MAX_CHUNKS,),
        in_specs=[pl.BlockSpec((CH, D), rows),
                  pl.BlockSpec((None, D, FF), wmap),
                  pl.BlockSpec((None, D, FF), wmap),
                  pl.BlockSpec((None, FF, D), wmap)],
        out_specs=pl.BlockSpec((CH, D), rows),
        scratch_shapes=[pltpu.VMEM((D, FF), BF16), pltpu.VMEM((D, FF), BF16), pltpu.VMEM((FF, D), BF16)],
    )
    return pl.pallas_call(
        _expert_kernel,
        grid_spec=grid_spec,
        out_shape=jax.ShapeDtypeStruct((XS_ROWS, D), F32),
        input_output_aliases={2: 0},
        compiler_params=_params("arbitrary"),
        name="moe_experts",
    )(ce, nc, xs, wg, wu, wd)


CB_TM = 256


def _combine_kernel(eid_ref, rank_ref, offs_ref, y_ref, x_ref, mw_ref, mod_ref, g_ref, o_ref, ybuf, sem):
    base = pl.program_id(0) * CB_TM

    def issue(r, carry):
        for k in range(2):
            p = (base + r) * 2 + k
            slot = offs_ref[eid_ref[p]] + rank_ref[p]
            _row_copy(y_ref, slot, ybuf.at[k], r, sem).start()
        return carry

    lax.fori_loop(0, CB_TM, issue, 0)

    def drain(r, carry):
        for k in range(2):
            _row_copy(y_ref, 0, ybuf.at[k], 0, sem).wait()
        return carry

    lax.fori_loop(0, CB_TM, drain, 0)

    moe = mw_ref[:, 0:1] * ybuf[0] + mw_ref[:, 1:2] * ybuf[1]
    x = x_ref[...] + mod_ref[5:6, :] * moe
    r = lax.rsqrt(jnp.mean(x * x, axis=-1, keepdims=True) + EPS)
    o_ref[...] = (x * r) * g_ref[...]


def _combine(eid, rank, offs, y, x1, mw, mod3, g):
    tpb = S // CB_TM
    grid_spec = pltpu.PrefetchScalarGridSpec(
        num_scalar_prefetch=3,
        grid=(N // CB_TM,),
        in_specs=[pl.BlockSpec(memory_space=pl.ANY),
                  pl.BlockSpec((CB_TM, D), lambda i, *_: (i, 0)),
                  pl.BlockSpec((CB_TM, LANES), lambda i, *_: (i, 0)),
                  pl.BlockSpec((None, N_MOD, D), lambda i, *_: (i // tpb, 0, 0)),
                  pl.BlockSpec((1, D), lambda i, *_: (0, 0))],
        out_specs=pl.BlockSpec((CB_TM, D), lambda i, *_: (i, 0)),
        scratch_shapes=[pltpu.VMEM((2, CB_TM, D), F32), pltpu.SemaphoreType.DMA(())],
    )
    return pl.pallas_call(
        _combine_kernel,
        grid_spec=grid_spec,
        out_shape=jax.ShapeDtypeStruct((N, D), F32),
        compiler_params=_params("arbitrary"),
        name="moe_combine_norm",
    )(eid, rank, offs, y, x1, mw, mod3, g.reshape(1, D))


def _pad_heads(w):
    g = w.shape[1] // HEAD_DIM
    w3 = w.reshape(D, g, HEAD_DIM)
    z = jnp.zeros_like(w3)
    return jnp.concatenate([w3, z, z, w3], axis=2).reshape(D, g * 4 * HEAD_DIM)


def kernel(x, c, w_ada, b_ada, g_norm_mix, w_in, conv_w, w_a_up, w_b_out, w_o, g_norm_ffn,
           w_rg, b_rg, w_re, b_re, w_e_gate, w_e_up, w_e_down, g_norm_final):
    x2 = x.reshape(N, D)
    mod3 = _ada(c, w_ada[0], b_ada[0]).reshape(B, N_MOD, D)

    wi = w_in[0]
    o_q, o_k, o_v, o_iq, o_ik, o_iw, o_bg = 0, 1024, 1280, 1536, 2560, 2624, 2640
    w_a = jnp.concatenate([wi[:, o_q:o_k], wi[:, o_iq:o_ik], _pad_heads(wi[:, o_k:o_v]),
                           wi[:, o_v:o_iq], _pad_heads(wi[:, o_ik:o_iw])], axis=1).astype(BF16)
    w_iw = jnp.pad(wi[:, o_iw:o_bg], ((0, 0), (0, LANES - IDX_HEADS))).astype(BF16)
    w_b = wi[:, o_bg:].astype(BF16)

    h = _norm1(x2, g_norm_mix[0], mod3)
    pa = _matmul(h, w_a, BF16, 1024, 512, "proj_attn")
    iw = _matmul(h, w_iw, F32, 1024, LANES, "proj_iw")
    pb = _matmul(h, w_b, F32, 1024, 512, "proj_conv_gates")

    oa = _attention(pa, iw)
    zb = _short_conv(pb, conv_w[0])
    merged = _merge(oa, zb, w_a_up[0].astype(BF16), w_b_out[0].astype(BF16), pb)
    x1 = _out_proj(merged, w_o[0].astype(BF16), x2, mod3)

    wr = jnp.pad(jnp.concatenate([w_rg[0], w_re[0]], axis=1),
                 ((0, 0), (0, LANES - N_GROUPS - N_EXP))).astype(BF16)
    br = jnp.pad(jnp.concatenate([b_rg[0], b_re[0]]), (0, LANES - N_GROUPS - N_EXP)).reshape(1, LANES)
    h2, mi, mw, cnt = _router(x1, g_norm_ffn[0], mod3, wr, br)

    counts = cnt[0, :N_EXP].astype(jnp.int32)
    nch = (counts + CH - 1) // CH
    cum = jnp.cumsum(nch)
    offs = ((cum - nch) * CH).astype(jnp.int32)
    total = cum[-1]
    cidx = jnp.minimum(jnp.arange(MAX_CHUNKS, dtype=jnp.int32), total - 1)
    ce = jnp.minimum(jnp.sum((cum[None, :] <= cidx[:, None]).astype(jnp.int32), axis=1), N_EXP - 1)
    nc = total.reshape(1).astype(jnp.int32)
    eid = mi[:, 0:2].reshape(-1)
    rank = mi[:, 2:4].reshape(-1)

    xs = _dispatch(eid, rank, offs, h2, jnp.zeros((XS_ROWS, D), F32))
    y = _experts(ce, nc, xs, w_e_gate[0], w_e_up[0], w_e_down[0])
    out = _combine(eid, rank, offs, y, x1, mw, mod3, g_norm_final)
    return out.reshape(B, S, D)
```

```python
import functools

import jax
import jax.numpy as jnp
from jax import lax
from jax.experimental import pallas as pl
from jax.experimental.pallas import tpu as pltpu

F32 = jnp.float32
BF16 = jnp.bfloat16

D = 2048
B = 2
S = 4096
N = B * S
EPS = 1e-6
HEAD_DIM = 64
ATTN_W = 1024
N_KV = 4
KV_W = 256
IDX_HEADS = 16
IDX_DIM = 64
TOPK = min(256, S // 4)
CONV_W = 1024
N_GROUPS = 4
EPG = 8
N_EXP = 32
FF = 512
N_MOD = 6

LANES = 128
TQ = 128
TK = 512
CH = 576
MAX_CHUNKS = (N * 2) // CH + N_EXP
XS_ROWS = MAX_CHUNKS * CH
MASKED = -1e30
BISECT_MAX_IT = 64
CNT_ROWS = 64
VMEM_LIMIT = 56 * 1024 * 1024


def _dot(a, b):
    return jnp.dot(a, b, preferred_element_type=F32)


def _dot_nt(a, b):
    return lax.dot_general(a, b, (((1,), (1,)), ((), ())), preferred_element_type=F32)


def _params(*sem):
    return pltpu.CompilerParams(dimension_semantics=sem, vmem_limit_bytes=VMEM_LIMIT)


ADA_TN = 1024
ADA_KC = 256


def _ada_kernel(ct_ref, w_ref, b_ref, o_ref):
    tn = w_ref.shape[1]

    def body(i, accs):
        k0 = pl.multiple_of(i * ADA_KC, ADA_KC)
        ct = ct_ref[pl.ds(k0, ADA_KC), :]
        act = ct * jax.nn.sigmoid(ct)
        w = w_ref[pl.ds(k0, ADA_KC), :]
        out = []
        for b in range(B):
            prod = w * act[:, b:b + 1]
            out.append(accs[b] + prod.reshape(ADA_KC // 8, 8, tn).sum(axis=0))
        return tuple(out)

    accs = lax.fori_loop(0, D // ADA_KC, body, tuple(jnp.zeros((8, tn), F32) for _ in range(B)))
    rows = [jnp.sum(a, axis=0, keepdims=True) for a in accs]
    o_ref[...] = jnp.concatenate(rows, axis=0) + b_ref[...]


def _ada(c, w, b):
    n_out = w.shape[1]
    return pl.pallas_call(
        _ada_kernel,
        grid=(n_out // ADA_TN,),
        in_specs=[pl.BlockSpec((D, B), lambda j: (0, 0)),
                  pl.BlockSpec((D, ADA_TN), lambda j: (0, j)),
                  pl.BlockSpec((1, ADA_TN), lambda j: (0, j))],
        out_specs=pl.BlockSpec((B, ADA_TN), lambda j: (0, j)),
        out_shape=jax.ShapeDtypeStruct((B, n_out), F32),
        compiler_params=_params("arbitrary"),
        name="ada_mod",
    )(c.T, w, b.reshape(1, n_out))


NORM_TM = 512


def _modulated_norm(x, g, shift, scale):
    r = lax.rsqrt(jnp.mean(x * x, axis=-1, keepdims=True) + EPS)
    return (x * r) * g * (1.0 + scale) + shift


def _norm1_kernel(x_ref, g_ref, mod_ref, h_ref):
    h = _modulated_norm(x_ref[...], g_ref[...], mod_ref[0:1, :], mod_ref[1:2, :])
    h_ref[...] = h.astype(h_ref.dtype)


def _norm1(x2, g, mod3):
    tpb = S // NORM_TM
    return pl.pallas_call(
        _norm1_kernel,
        grid=(N // NORM_TM,),
        in_specs=[pl.BlockSpec((NORM_TM, D), lambda i: (i, 0)),
                  pl.BlockSpec((1, D), lambda i: (0, 0)),
                  pl.BlockSpec((None, N_MOD, D), lambda i: (i // tpb, 0, 0))],
        out_specs=pl.BlockSpec((NORM_TM, D), lambda i: (i, 0)),
        out_shape=jax.ShapeDtypeStruct((N, D), BF16),
        compiler_params=_params("arbitrary"),
        name="norm1",
    )(x2, g.reshape(1, D), mod3)


def _mm_kernel(a_ref, w_ref, o_ref):
    o_ref[...] = _dot(a_ref[...], w_ref[...]).astype(o_ref.dtype)


def _matmul(a, w, out_dtype, tm, tn, name):
    m, k = a.shape
    n = w.shape[1]
    return pl.pallas_call(
        _mm_kernel,
        grid=(m // tm, n // tn),
        in_specs=[pl.BlockSpec((tm, k), lambda i, j: (i, 0)),
                  pl.BlockSpec((k, tn), lambda i, j: (0, j))],
        out_specs=pl.BlockSpec((tm, tn), lambda i, j: (i, j)),
        out_shape=jax.ShapeDtypeStruct((m, n), out_dtype),
        compiler_params=_params("arbitrary", "arbitrary"),
        name=name,
    )(a, w)


def _mm_nt_kernel(w_ref, a_ref, o_ref):
    o_ref[...] = _dot_nt(w_ref[...], a_ref[...]).astype(o_ref.dtype)


def _matmul_nt(w, a, out_dtype, tm, name):
    r, k = w.shape
    m = a.shape[0]
    return pl.pallas_call(
        _mm_nt_kernel,
        grid=(m // tm,),
        in_specs=[pl.BlockSpec((r, k), lambda i: (0, 0)),
                  pl.BlockSpec((tm, k), lambda i: (i, 0))],
        out_specs=pl.BlockSpec((r, tm), lambda i: (0, i)),
        out_shape=jax.ShapeDtypeStruct((r, m), out_dtype),
        compiler_params=_params("arbitrary"),
        name=name,
    )(w, a)


A_COLS = 2 * ATTN_W + 4 * KV_W + KV_W + 4 * IDX_DIM


def _attn_kernel(q_ref, iq_ref, iwt_ref, ikp_ref, kp_ref, v_ref, o_ref,
                 s_ref, thr_ref, qs_ref, iqs_ref, vaug_ref, m_ref, acc_ref, a_ref, lg_ref):
    qi = pl.program_id(1)
    nkc = qi // (TK // TQ) + 1

    @pl.when(qi == 0)
    def _():
        lane_v = lax.broadcasted_iota(jnp.int32, (TK, LANES), 1)
        ones = jnp.ones((TK, LANES), BF16)

        def fill(c, carry):
            off = pl.multiple_of(c * TK, TK)
            for g in range(N_KV):
                src = v_ref[pl.ds(off, TK), LANES * (g // 2):LANES * (g // 2 + 1)]
                keep = (lane_v < 64) if g % 2 == 0 else (lane_v >= 64)
                vaug_ref[pl.ds(off, TK), LANES * g:LANES * (g + 1)] = jnp.where(keep, src, ones)
            return carry

        lax.fori_loop(0, S // TK, fill, 0)

    ident = jnp.where(lax.broadcasted_iota(jnp.int32, (TQ, LANES), 0)
                      == lax.broadcasted_iota(jnp.int32, (TQ, LANES), 1), 1.0, 0.0).astype(BF16)
    for j in range(8):
        iqs_ref[j * TQ:(j + 1) * TQ, :] = iq_ref[:, LANES * j:LANES * (j + 1)]
        qs_ref[j * TQ:(j + 1) * TQ, 0:LANES] = q_ref[:, LANES * j:LANES * (j + 1)] * (HEAD_DIM ** -0.5)
        qs_ref[j * TQ:(j + 1) * TQ, LANES:2 * LANES] = ident

    w = iwt_ref[0:IDX_HEADS, :] * ((IDX_HEADS ** -0.5) * (IDX_DIM ** -0.5))

    keyi = lax.broadcasted_iota(jnp.int32, (TK, TQ), 0)
    qryi = qi * TQ + lax.broadcasted_iota(jnp.int32, (TK, TQ), 1)

    def score_body(c, carry):
        lo, hi = carry
        off = pl.multiple_of(c * TK, TK)
        iqs = iqs_ref[...]
        d_even = _dot_nt(ikp_ref[pl.ds(off, TK), 0:LANES], iqs)
        d_odd = _dot_nt(ikp_ref[pl.ds(off, TK), LANES:2 * LANES], iqs)
        acc = jnp.zeros((TK, TQ), F32)
        for j in range(8):
            acc = acc + jnp.maximum(d_even[:, j * TQ:(j + 1) * TQ], 0.0) * w[2 * j:2 * j + 1, :]
            acc = acc + jnp.maximum(d_odd[:, j * TQ:(j + 1) * TQ], 0.0) * w[2 * j + 1:2 * j + 2, :]
        valid = (off + keyi) <= qryi
        s_ref[pl.ds(off, TK), :] = jnp.where(valid, acc, -jnp.inf)
        hi = jnp.maximum(hi, jnp.max(jnp.where(valid, acc, -jnp.inf), axis=0, keepdims=True))
        lo = jnp.minimum(lo, jnp.min(jnp.where(valid, acc, jnp.inf), axis=0, keepdims=True))
        return lo, hi

    lo0, hi0 = lax.fori_loop(0, nkc, score_body,
                             (jnp.full((1, TQ), jnp.inf, F32), jnp.full((1, TQ), -jnp.inf, F32)))

    def count_ge(mid):
        def body(c, cnt):
            off = pl.multiple_of(c * TK, TK)
            ge = jnp.where(s_ref[pl.ds(off, TK), :] >= mid, 1.0, 0.0)
            return cnt + ge.reshape(TK // CNT_ROWS, CNT_ROWS, TQ).sum(axis=0)

        cnt = lax.fori_loop(0, nkc, body, jnp.zeros((CNT_ROWS, TQ), F32))
        return jnp.sum(cnt, axis=0, keepdims=True)

    @pl.when(qi * TQ + TQ <= TOPK)
    def _():
        thr_ref[...] = jnp.full((8, TQ), jnp.finfo(F32).min, F32)

    @pl.when(qi * TQ + TQ > TOPK)
    def _():
        kf = float(TOPK)

        def step(lo, hi):
            mid = lo + (hi - lo) * 0.5
            cnt = count_ge(mid)
            ge = cnt >= kf
            eq = cnt == kf
            lo2 = jnp.where(ge, mid, lo)
            hi2 = jnp.where(eq, mid, jnp.where(ge, hi, mid))
            stuck = jnp.logical_or(mid <= lo, mid >= hi)
            return lo2, hi2, jnp.logical_and(lo2 < hi2, jnp.logical_not(stuck))

        def cond(st):
            it, _, _, active = st
            return jnp.logical_and(it < BISECT_MAX_IT, active > 0.0)

        def body(st):
            it, lo, hi, _ = st
            lo, hi, _ = step(lo, hi)
            lo, hi, act = step(lo, hi)
            return it + 2, lo, hi, jnp.max(jnp.where(act, 1.0, 0.0))

        _, lo, _, _ = lax.while_loop(cond, body, (jnp.int32(0), lo0, hi0, jnp.float32(1.0)))
        thr_ref[...] = jnp.broadcast_to(lo, (8, TQ))

    m_ref[...] = jnp.full(m_ref.shape, MASKED, F32)
    acc_ref[...] = jnp.zeros(acc_ref.shape, F32)
    thr = thr_ref[0:1, :]
    nt = TK // LANES
    n_blk = 2 * N_KV

    def attn_body(c, carry):
        off = pl.multiple_of(c * TK, TK)
        bias_t = jnp.where(s_ref[pl.ds(off, TK), :] >= thr, 0.0, MASKED).astype(BF16)

        for blk in range(n_blk):
            g, par = blk // 2, blk % 2
            qg = qs_ref[g * 2 * TQ:(g + 1) * 2 * TQ, :]
            kk = kp_ref[pl.ds(off, TK), g * 256 + par * LANES:g * 256 + (par + 1) * LANES]
            lg = _dot_nt(qg, jnp.concatenate([kk, bias_t], axis=1))
            lg_ref[blk] = lg
            m_old = m_ref[blk]
            m_new = jnp.maximum(m_old, jnp.max(lg, axis=1, keepdims=True))
            a_ref[blk] = jnp.exp(m_old - m_new)
            m_ref[blk] = m_new

        for blk in range(n_blk):
            g = blk // 2
            m_new = m_ref[blk]
            p = jnp.concatenate([jnp.exp(lg_ref[blk, :, LANES * t:LANES * (t + 1)] - m_new)
                                 for t in range(nt)], axis=1).astype(BF16)
            va = vaug_ref[pl.ds(off, TK), LANES * g:LANES * (g + 1)]
            acc_ref[blk] = a_ref[blk] * acc_ref[blk] + _dot(p, va)
        return carry

    lax.fori_loop(0, nkc, attn_body, 0)

    lane = lax.broadcasted_iota(jnp.int32, (TQ, LANES), 1)
    for g in range(N_KV):
        acc_e = acc_ref[2 * g]
        acc_o = acc_ref[2 * g + 1]
        o_even = acc_e / pltpu.roll(acc_e, 64, axis=1)
        o_odd = acc_o / pltpu.roll(acc_o, 64, axis=1)
        for pr in range(2):
            xa = o_even[pr * TQ:(pr + 1) * TQ]
            xb = o_odd[pr * TQ:(pr + 1) * TQ]
            if g % 2 == 0:
                left, right = xa, pltpu.roll(xb, 64, axis=1)
            else:
                left, right = pltpu.roll(xa, 64, axis=1), xb
            col = (2 * g + pr) * LANES
            o_ref[:, col:col + LANES] = jnp.where(lane < 64, left, right).astype(o_ref.dtype)


def _attention(a, iwt):
    nq = S // TQ
    rowmap = lambda b, i: b * nq + i
    return pl.pallas_call(
        _attn_kernel,
        grid=(B, nq),
        in_specs=[pl.BlockSpec((TQ, ATTN_W), lambda b, i: (rowmap(b, i), 0)),
                  pl.BlockSpec((TQ, ATTN_W), lambda b, i: (rowmap(b, i), 1)),
                  pl.BlockSpec((LANES, TQ), lambda b, i: (0, rowmap(b, i))),
                  pl.BlockSpec((S, 256), lambda b, i: (b, 13)),
                  pl.BlockSpec((S, 1024), lambda b, i: (b, 2)),
                  pl.BlockSpec((S, 256), lambda b, i: (b, 12))],
        out_specs=pl.BlockSpec((TQ, ATTN_W), lambda b, i: (rowmap(b, i), 0)),
        out_shape=jax.ShapeDtypeStruct((N, ATTN_W), BF16),
        scratch_shapes=[pltpu.VMEM((S, TQ), F32),
                        pltpu.VMEM((8, TQ), F32),
                        pltpu.VMEM((8 * TQ, 2 * LANES), BF16),
                        pltpu.VMEM((8 * TQ, LANES), BF16),
                        pltpu.VMEM((S, N_KV * LANES), BF16),
                        pltpu.VMEM((2 * N_KV, 2 * TQ, LANES), F32),
                        pltpu.VMEM((2 * N_KV, 2 * TQ, LANES), F32),
                        pltpu.VMEM((2 * N_KV, 2 * TQ, LANES), F32),
                        pltpu.VMEM((2 * N_KV, 2 * TQ, TK), F32)],
        compiler_params=_params("arbitrary", "arbitrary"),
        name="sparse_attn",
    )(a, a, iwt, a, a, a)


CONV_TS = 512


def _conv_kernel(bg_ref, cg_ref, xc_ref, cw_ref, o_ref, carry_ref):
    i = pl.program_id(0)

    @pl.when(i % (S // CONV_TS) == 0)
    def _():
        carry_ref[...] = jnp.zeros(carry_ref.shape, F32)

    u = cg_ref[...] * xc_ref[...]
    prev = carry_ref[...]
    row = lax.broadcasted_iota(jnp.int32, u.shape, 0)
    u1 = jnp.where(row == 0, prev[7:8, :], pltpu.roll(u, 1, axis=0))
    u2 = jnp.where(row == 0, prev[6:7, :], jnp.where(row == 1, prev[7:8, :], pltpu.roll(u, 2, axis=0)))
    y = cw_ref[0:1, :] * u2 + cw_ref[1:2, :] * u1 + cw_ref[2:3, :] * u
    o_ref[...] = (bg_ref[...] * y).astype(o_ref.dtype)
    carry_ref[...] = u[CONV_TS - 8:CONV_TS, :]


def _short_conv(pb, conv_w):
    return pl.pallas_call(
        _conv_kernel,
        grid=(N // CONV_TS,),
        in_specs=[pl.BlockSpec((CONV_TS, CONV_W), lambda i: (i, 0)),
                  pl.BlockSpec((CONV_TS, CONV_W), lambda i: (i, 1)),
                  pl.BlockSpec((CONV_TS, CONV_W), lambda i: (i, 2)),
                  pl.BlockSpec((3, CONV_W), lambda i: (0, 0))],
        out_specs=pl.BlockSpec((CONV_TS, CONV_W), lambda i: (i, 0)),
        out_shape=jax.ShapeDtypeStruct((N, CONV_W), BF16),
        scratch_shapes=[pltpu.VMEM((8, CONV_W), F32)],
        compiler_params=_params("arbitrary"),
        name="short_conv",
    )(pb, pb, pb, conv_w)


MG_TM = 1024
MG_TN = 512


def _merge_kernel(oa_ref, zb_ref, wa_ref, wb_ref, ga_ref, gb_ref, o_ref):
    ya = _dot(oa_ref[...], wa_ref[...])
    yb = _dot(zb_ref[...], wb_ref[...])
    merged = jax.nn.sigmoid(ga_ref[...]) * ya + jax.nn.sigmoid(gb_ref[...]) * yb
    o_ref[...] = merged.astype(o_ref.dtype)


def _merge(oa, zb, wa, wb, pb):
    ga0 = (3 * CONV_W) // MG_TN
    gb0 = (3 * CONV_W + D) // MG_TN
    return pl.pallas_call(
        _merge_kernel,
        grid=(N // MG_TM, D // MG_TN),
        in_specs=[pl.BlockSpec((MG_TM, ATTN_W), lambda i, j: (i, 0)),
                  pl.BlockSpec((MG_TM, CONV_W), lambda i, j: (i, 0)),
                  pl.BlockSpec((ATTN_W, MG_TN), lambda i, j: (0, j)),
                  pl.BlockSpec((CONV_W, MG_TN), lambda i, j: (0, j)),
                  pl.BlockSpec((MG_TM, MG_TN), lambda i, j: (i, ga0 + j)),
                  pl.BlockSpec((MG_TM, MG_TN), lambda i, j: (i, gb0 + j))],
        out_specs=pl.BlockSpec((MG_TM, MG_TN), lambda i, j: (i, j)),
        out_shape=jax.ShapeDtypeStruct((N, D), BF16),
        compiler_params=_params("arbitrary", "arbitrary"),
        name="gated_merge",
    )(oa, zb, wa, wb, pb, pb)


def _wo_kernel(m_ref, w_ref, x_ref, mod_ref, o_ref):
    o_ref[...] = x_ref[...] + mod_ref[2:3, :] * _dot(m_ref[...], w_ref[...])


def _out_proj(merged, wo, x2, mod3):
    tpb = S // MG_TM
    return pl.pallas_call(
        _wo_kernel,
        grid=(N // MG_TM, D // MG_TN),
        in_specs=[pl.BlockSpec((MG_TM, D), lambda i, j: (i, 0)),
                  pl.BlockSpec((D, MG_TN), lambda i, j: (0, j)),
                  pl.BlockSpec((MG_TM, MG_TN), lambda i, j: (i, j)),
                  pl.BlockSpec((None, N_MOD, MG_TN), lambda i, j: (i // tpb, 0, j))],
        out_specs=pl.BlockSpec((MG_TM, MG_TN), lambda i, j: (i, j)),
        out_shape=jax.ShapeDtypeStruct((N, D), F32),
        compiler_params=_params("arbitrary", "arbitrary"),
        name="out_proj",
    )(merged, wo, x2, mod3)


RT_TM = 256


def _router_kernel(x_ref, g_ref, mod_ref, wr_ref, br_ref,
                   h_ref, mi_ref, mw_ref, cnt_ref, carry_ref):
    i = pl.program_id(0)

    @pl.when(i == 0)
    def _():
        carry_ref[...] = jnp.zeros(carry_ref.shape, F32)

    h = _modulated_norm(x_ref[...], g_ref[...], mod_ref[3:4, :], mod_ref[4:5, :])
    h_ref[...] = h
    lg = _dot(h.astype(BF16), wr_ref[...]) + br_ref[...]
    lane = lax.broadcasted_iota(jnp.int32, lg.shape, 1)
    ninf = -jnp.inf

    gl = jnp.where(lane < N_GROUPS, lg, ninf)
    gm = jnp.max(gl, axis=1, keepdims=True)
    g_sel = jnp.min(jnp.where(gl == gm, lane, LANES), axis=1, keepdims=True)
    p_g = 1.0 / jnp.sum(jnp.exp(gl - gm), axis=1, keepdims=True)

    grp = jnp.right_shift(lane - N_GROUPS, 3)
    in_grp = jnp.logical_and(jnp.logical_and(lane >= N_GROUPS, lane < N_GROUPS + N_EXP), grp == g_sel)
    el = jnp.where(in_grp, lg, ninf)
    m1 = jnp.max(el, axis=1, keepdims=True)
    i1 = jnp.min(jnp.where(el == m1, lane, LANES), axis=1, keepdims=True)
    el2 = jnp.where(lane == i1, ninf, el)
    m2 = jnp.max(el2, axis=1, keepdims=True)
    i2 = jnp.min(jnp.where(el2 == m2, lane, LANES), axis=1, keepdims=True)
    e = jnp.exp(m2 - m1)
    w1 = p_g * (1.0 / (1.0 + e))
    w2 = p_g * (e / (1.0 + e))
    e1 = i1 - N_GROUPS
    e2 = i2 - N_GROUPS

    oh1 = lane == e1
    oh2 = lane == e2
    used = jnp.where(jnp.logical_or(oh1, oh2), 1.0, 0.0)
    ri = lax.broadcasted_iota(jnp.int32, (RT_TM, RT_TM), 0)
    ci = lax.broadcasted_iota(jnp.int32, (RT_TM, RT_TM), 1)
    lower = jnp.where(ci < ri, 1.0, 0.0).astype(BF16)
    before = _dot(lower, used.astype(BF16)) + carry_ref[...]
    r1 = jnp.sum(jnp.where(oh1, before, 0.0), axis=1, keepdims=True).astype(jnp.int32)
    r2 = jnp.sum(jnp.where(oh2, before, 0.0), axis=1, keepdims=True).astype(jnp.int32)
    total = carry_ref[...] + jnp.sum(used, axis=0, keepdims=True)
    carry_ref[...] = total
    cnt_ref[...] = total

    zero = jnp.zeros_like(lane)
    mi_ref[...] = jnp.where(lane == 0, e1, jnp.where(lane == 1, e2,
                            jnp.where(lane == 2, r1, jnp.where(lane == 3, r2, zero))))
    mw_ref[...] = jnp.where(lane == 0, w1, jnp.where(lane == 1, w2, 0.0))


def _router(x1, g, mod3, wr, br):
    tpb = S // RT_TM
    return pl.pallas_call(
        _router_kernel,
        grid=(N // RT_TM,),
        in_specs=[pl.BlockSpec((RT_TM, D), lambda i: (i, 0)),
                  pl.BlockSpec((1, D), lambda i: (0, 0)),
                  pl.BlockSpec((None, N_MOD, D), lambda i: (i // tpb, 0, 0)),
                  pl.BlockSpec((D, LANES), lambda i: (0, 0)),
                  pl.BlockSpec((1, LANES), lambda i: (0, 0))],
        out_specs=[pl.BlockSpec((RT_TM, D), lambda i: (i, 0)),
                   pl.BlockSpec((RT_TM, LANES), lambda i: (i, 0)),
                   pl.BlockSpec((RT_TM, LANES), lambda i: (i, 0)),
                   pl.BlockSpec((1, LANES), lambda i: (0, 0))],
        out_shape=[jax.ShapeDtypeStruct((N, D), F32),
                   jax.ShapeDtypeStruct((N, LANES), jnp.int32),
                   jax.ShapeDtypeStruct((N, LANES), F32),
                   jax.ShapeDtypeStruct((1, LANES), F32)],
        scratch_shapes=[pltpu.VMEM((1, LANES), F32)],
        compiler_params=_params("arbitrary"),
        name="norm2_router",
    )(x1, g.reshape(1, D), mod3, wr, br)


DP_TM = 256


def _row_copy(src_ref, src_row, dst_ref, dst_row, sem):
    return pltpu.make_async_copy(src_ref.at[pl.ds(src_row, 1), :], dst_ref.at[pl.ds(dst_row, 1), :], sem)


def _dispatch_kernel(slot_ref, h_ref, xs_in_ref, xs_ref, sem):
    del xs_in_ref
    base = pl.program_id(0) * DP_TM

    def issue(r, carry):
        for k in range(2):
            _row_copy(h_ref, r, xs_ref, slot_ref[(base + r) * 2 + k], sem).start()
        return carry

    lax.fori_loop(0, DP_TM, issue, 0, unroll=8)

    def drain(r, carry):
        for _ in range(2):
            _row_copy(h_ref, 0, xs_ref, 0, sem).wait()
        return carry

    lax.fori_loop(0, DP_TM, drain, 0, unroll=8)


def _dispatch(slot, h2, xs0):
    grid_spec = pltpu.PrefetchScalarGridSpec(
        num_scalar_prefetch=1,
        grid=(N // DP_TM,),
        in_specs=[pl.BlockSpec((DP_TM, D), lambda i, *_: (i, 0)),
                  pl.BlockSpec(memory_space=pl.ANY)],
        out_specs=pl.BlockSpec(memory_space=pl.ANY),
        scratch_shapes=[pltpu.SemaphoreType.DMA(())],
    )
    return pl.pallas_call(
        _dispatch_kernel,
        grid_spec=grid_spec,
        out_shape=jax.ShapeDtypeStruct((XS_ROWS, D), F32),
        input_output_aliases={2: 0},
        compiler_params=_params("arbitrary"),
        name="moe_dispatch",
    )(slot, h2, xs0)


def _expert_kernel(ce_ref, nc_ref, x_ref, wg_ref, wu_ref, wd_ref, y_ref, wg_s, wu_s, wd_s):
    c = pl.program_id(0)
    valid = c < nc_ref[0]
    new_expert = jnp.logical_or(c == 0, ce_ref[c] != ce_ref[jnp.maximum(c - 1, 0)])

    @pl.when(jnp.logical_and(valid, new_expert))
    def _():
        wg_s[...] = wg_ref[...].astype(BF16)
        wu_s[...] = wu_ref[...].astype(BF16)
        wd_s[...] = wd_ref[...].astype(BF16)

    @pl.when(valid)
    def _():
        x = x_ref[...].astype(BF16)
        a = _dot(x, wg_s[...])
        u = _dot(x, wu_s[...])
        act = (a * jax.nn.sigmoid(a)) * u
        y_ref[...] = _dot(act.astype(BF16), wd_s[...])


def _experts(ce, nc, xs, wg, wu, wd):
    rows = lambda c, ce, nc: (jnp.minimum(c, nc[0] - 1), 0)
    wmap = lambda c, ce, nc: (ce[c], 0, 0)
    grid_spec = pltpu.PrefetchScalarGridSpec(
        num_scalar_prefetch=2,
        grid=(MAX_CHUNKS,),
        in_specs=[pl.BlockSpec((CH, D), rows),
                  pl.BlockSpec((None, D, FF), wmap),
                  pl.BlockSpec((None, D, FF), wmap),
                  pl.BlockSpec((None, FF, D), wmap)],
        out_specs=pl.BlockSpec((CH, D), rows),
        scratch_shapes=[pltpu.VMEM((D, FF), BF16), pltpu.VMEM((D, FF), BF16), pltpu.VMEM((FF, D), BF16)],
    )
    return pl.pallas_call(
        _expert_kernel,
        grid_spec=grid_spec,
        out_shape=jax.ShapeDtypeStruct((XS_ROWS, D), F32),
        input_output_aliases={2: 0},
        compiler_params=_params("arbitrary"),
        name="moe_experts",
    )(ce, nc, xs, wg, wu, wd)


CB_TM = 256


def _combine_kernel(slot_ref, y_ref, x_ref, mw_ref, mod_ref, g_ref, o_ref, ybuf, sem):
    base = pl.program_id(0) * CB_TM

    def issue(r, carry):
        for k in range(2):
            _row_copy(y_ref, slot_ref[(base + r) * 2 + k], ybuf.at[k], r, sem).start()
        return carry

    lax.fori_loop(0, CB_TM, issue, 0, unroll=8)

    def drain(r, carry):
        for k in range(2):
            _row_copy(y_ref, 0, ybuf.at[k], 0, sem).wait()
        return carry

    lax.fori_loop(0, CB_TM, drain, 0, unroll=8)

    moe = mw_ref[:, 0:1] * ybuf[0] + mw_ref[:, 1:2] * ybuf[1]
    x = x_ref[...] + mod_ref[5:6, :] * moe
    r = lax.rsqrt(jnp.mean(x * x, axis=-1, keepdims=True) + EPS)
    o_ref[...] = (x * r) * g_ref[...]


def _combine(slot, y, x1, mw, mod3, g):
    tpb = S // CB_TM
    grid_spec = pltpu.PrefetchScalarGridSpec(
        num_scalar_prefetch=1,
        grid=(N // CB_TM,),
        in_specs=[pl.BlockSpec(memory_space=pl.ANY),
                  pl.BlockSpec((CB_TM, D), lambda i, *_: (i, 0)),
                  pl.BlockSpec((CB_TM, LANES), lambda i, *_: (i, 0)),
                  pl.BlockSpec((None, N_MOD, D), lambda i, *_: (i // tpb, 0, 0)),
                  pl.BlockSpec((1, D), lambda i, *_: (0, 0))],
        out_specs=pl.BlockSpec((CB_TM, D), lambda i, *_: (i, 0)),
        scratch_shapes=[pltpu.VMEM((2, CB_TM, D), F32), pltpu.SemaphoreType.DMA(())],
    )
    return pl.pallas_call(
        _combine_kernel,
        grid_spec=grid_spec,
        out_shape=jax.ShapeDtypeStruct((N, D), F32),
        compiler_params=_params("arbitrary"),
        name="moe_combine_norm",
    )(slot, y, x1, mw, mod3, g.reshape(1, D))


def _pad_heads(w):
    g = w.shape[1] // HEAD_DIM
    w3 = w.reshape(D, g, HEAD_DIM)
    z = jnp.zeros_like(w3)
    return jnp.concatenate([w3, z, z, w3], axis=2).reshape(D, g * 4 * HEAD_DIM)


def kernel(x, c, w_ada, b_ada, g_norm_mix, w_in, conv_w, w_a_up, w_b_out, w_o, g_norm_ffn,
           w_rg, b_rg, w_re, b_re, w_e_gate, w_e_up, w_e_down, g_norm_final):
    x2 = x.reshape(N, D)
    mod3 = _ada(c, w_ada[0], b_ada[0]).reshape(B, N_MOD, D)

    wi = w_in[0]
    o_q, o_k, o_v, o_iq, o_ik, o_iw, o_bg = 0, 1024, 1280, 1536, 2560, 2624, 2640
    w_a = jnp.concatenate([wi[:, o_q:o_k], wi[:, o_iq:o_ik], _pad_heads(wi[:, o_k:o_v]),
                           wi[:, o_v:o_iq], _pad_heads(wi[:, o_ik:o_iw])], axis=1).astype(BF16)
    w_iw_t = jnp.pad(wi[:, o_iw:o_bg].T, ((0, LANES - IDX_HEADS), (0, 0))).astype(BF16)
    w_b = wi[:, o_bg:].astype(BF16)

    h = _norm1(x2, g_norm_mix[0], mod3)
    pa = _matmul(h, w_a, BF16, 1024, 512, "proj_attn")
    iwt = _matmul_nt(w_iw_t, h, F32, 1024, "proj_iw")
    pb = _matmul(h, w_b, F32, 1024, 512, "proj_conv_gates")

    oa = _attention(pa, iwt)
    zb = _short_conv(pb, conv_w[0])
    merged = _merge(oa, zb, w_a_up[0].astype(BF16), w_b_out[0].astype(BF16), pb)
    x1 = _out_proj(merged, w_o[0].astype(BF16), x2, mod3)

    wr = jnp.pad(jnp.concatenate([w_rg[0], w_re[0]], axis=1),
                 ((0, 0), (0, LANES - N_GROUPS - N_EXP))).astype(BF16)
    br = jnp.pad(jnp.concatenate([b_rg[0], b_re[0]]), (0, LANES - N_GROUPS - N_EXP)).reshape(1, LANES)
    h2, mi, mw, cnt = _router(x1, g_norm_ffn[0], mod3, wr, br)

    counts = cnt[0, :N_EXP].astype(jnp.int32)
    nch = (counts + CH - 1) // CH
    cum = jnp.cumsum(nch)
    offs = ((cum - nch) * CH).astype(jnp.int32)
    total = cum[-1]
    cidx = jnp.minimum(jnp.arange(MAX_CHUNKS, dtype=jnp.int32), total - 1)
    ce = jnp.minimum(jnp.sum((cum[None, :] <= cidx[:, None]).astype(jnp.int32), axis=1), N_EXP - 1)
    nc = total.reshape(1).astype(jnp.int32)
    seg = jnp.sum(jnp.where(mi[:, 0:2, None] == jnp.arange(N_EXP, dtype=jnp.int32), offs, 0), axis=-1)
    slot = (seg + mi[:, 2:4]).reshape(-1)

    xs = _dispatch(slot, h2, jnp.zeros((XS_ROWS, D), F32))
    y = _experts(ce, nc, xs, w_e_gate[0], w_e_up[0], w_e_down[0])
    out = _combine(slot, y, x1, mw, mod3, g_norm_final)
    return out.reshape(B, S, D)
```

```python
import functools

import jax
import jax.numpy as jnp
from jax import lax
from jax.experimental import pallas as pl
from jax.experimental.pallas import tpu as pltpu

F32 = jnp.float32
BF16 = jnp.bfloat16

D = 2048
B = 2
S = 4096
N = B * S
EPS = 1e-6
HEAD_DIM = 64
ATTN_W = 1024
N_KV = 4
KV_W = 256
IDX_HEADS = 16
IDX_DIM = 64
TOPK = min(256, S // 4)
CONV_W = 1024
N_GROUPS = 4
EPG = 8
N_EXP = 32
FF = 512
N_MOD = 6

LANES = 128
TQ = 128
TK = 512
CH = 576
MAX_CHUNKS = (N * 2) // CH + N_EXP
XS_ROWS = MAX_CHUNKS * CH
MASKED = -1e30
BISECT_MAX_IT = 64
CNT_ROWS = 64
VMEM_LIMIT = 56 * 1024 * 1024


def _dot(a, b):
    return jnp.dot(a, b, preferred_element_type=F32)


def _dot_nt(a, b):
    return lax.dot_general(a, b, (((1,), (1,)), ((), ())), preferred_element_type=F32)


def _params(*sem):
    return pltpu.CompilerParams(dimension_semantics=sem, vmem_limit_bytes=VMEM_LIMIT)


ADA_TN = 1024
ADA_KC = 256


def _ada_kernel(ct_ref, w_ref, b_ref, o_ref):
    tn = w_ref.shape[1]

    def body(i, accs):
        k0 = pl.multiple_of(i * ADA_KC, ADA_KC)
        ct = ct_ref[pl.ds(k0, ADA_KC), :]
        act = ct * jax.nn.sigmoid(ct)
        w = w_ref[pl.ds(k0, ADA_KC), :]
        out = []
        for b in range(B):
            prod = w * act[:, b:b + 1]
            out.append(accs[b] + prod.reshape(ADA_KC // 8, 8, tn).sum(axis=0))
        return tuple(out)

    accs = lax.fori_loop(0, D // ADA_KC, body, tuple(jnp.zeros((8, tn), F32) for _ in range(B)))
    rows = [jnp.sum(a, axis=0, keepdims=True) for a in accs]
    o_ref[...] = jnp.concatenate(rows, axis=0) + b_ref[...]


def _ada(c, w, b):
    n_out = w.shape[1]
    return pl.pallas_call(
        _ada_kernel,
        grid=(n_out // ADA_TN,),
        in_specs=[pl.BlockSpec((D, B), lambda j: (0, 0)),
                  pl.BlockSpec((D, ADA_TN), lambda j: (0, j)),
                  pl.BlockSpec((1, ADA_TN), lambda j: (0, j))],
        out_specs=pl.BlockSpec((B, ADA_TN), lambda j: (0, j)),
        out_shape=jax.ShapeDtypeStruct((B, n_out), F32),
        compiler_params=_params("arbitrary"),
        name="ada_mod",
    )(c.T, w, b.reshape(1, n_out))


NORM_TM = 512


def _modulated_norm(x, g, shift, scale):
    r = lax.rsqrt(jnp.mean(x * x, axis=-1, keepdims=True) + EPS)
    return (x * r) * g * (1.0 + scale) + shift


def _norm_proj_kernel(x_ref, g_ref, mod_ref, wa_ref, wiw_ref, h_ref, pa_ref, iwt_ref):
    h = _modulated_norm(x_ref[...], g_ref[...], mod_ref[0:1, :], mod_ref[1:2, :]).astype(BF16)
    h_ref[...] = h
    pa_ref[...] = _dot(h, wa_ref[...]).astype(pa_ref.dtype)
    iwt_ref[...] = _dot_nt(wiw_ref[...], h)


def _norm_proj(x2, g, mod3, w_a, w_iw_t):
    tpb = S // NORM_TM
    n_a = w_a.shape[1]
    return pl.pallas_call(
        _norm_proj_kernel,
        grid=(N // NORM_TM,),
        in_specs=[pl.BlockSpec((NORM_TM, D), lambda i: (i, 0)),
                  pl.BlockSpec((1, D), lambda i: (0, 0)),
                  pl.BlockSpec((None, N_MOD, D), lambda i: (i // tpb, 0, 0)),
                  pl.BlockSpec((D, n_a), lambda i: (0, 0)),
                  pl.BlockSpec((LANES, D), lambda i: (0, 0))],
        out_specs=[pl.BlockSpec((NORM_TM, D), lambda i: (i, 0)),
                   pl.BlockSpec((NORM_TM, n_a), lambda i: (i, 0)),
                   pl.BlockSpec((LANES, NORM_TM), lambda i: (0, i))],
        out_shape=[jax.ShapeDtypeStruct((N, D), BF16),
                   jax.ShapeDtypeStruct((N, n_a), BF16),
                   jax.ShapeDtypeStruct((LANES, N), F32)],
        compiler_params=_params("arbitrary"),
        name="norm1_proj_attn",
    )(x2, g.reshape(1, D), mod3, w_a, w_iw_t)


def _mm_kernel(a_ref, w_ref, o_ref):
    o_ref[...] = _dot(a_ref[...], w_ref[...]).astype(o_ref.dtype)


def _matmul(a, w, out_dtype, tm, tn, name):
    m, k = a.shape
    n = w.shape[1]
    return pl.pallas_call(
        _mm_kernel,
        grid=(m // tm, n // tn),
        in_specs=[pl.BlockSpec((tm, k), lambda i, j: (i, 0)),
                  pl.BlockSpec((k, tn), lambda i, j: (0, j))],
        out_specs=pl.BlockSpec((tm, tn), lambda i, j: (i, j)),
        out_shape=jax.ShapeDtypeStruct((m, n), out_dtype),
        compiler_params=_params("arbitrary", "arbitrary"),
        name=name,
    )(a, w)


A_COLS = 2 * ATTN_W + 2 * KV_W + LANES


def _spread_matrix(groups):
    r = lax.broadcasted_iota(jnp.int32, (groups * HEAD_DIM, groups * 4 * HEAD_DIM), 0)
    c = lax.broadcasted_iota(jnp.int32, (groups * HEAD_DIM, groups * 4 * HEAD_DIM), 1)
    d, cc = r & (HEAD_DIM - 1), c & (4 * HEAD_DIM - 1)
    hit = jnp.logical_and(jnp.right_shift(r, 6) == jnp.right_shift(c, 8),
                          jnp.logical_or(cc == d, cc == d + 3 * HEAD_DIM))
    return jnp.where(hit, 1.0, 0.0).astype(BF16)


def _attn_kernel(q_ref, iq_ref, iwt_ref, ik_ref, k_ref, v_ref, o_ref,
                 s_ref, thr_ref, qs_ref, iqs_ref, ikp_ref, kp_ref, vaug_ref, m_ref, acc_ref, a_ref, lg_ref):
    qi = pl.program_id(1)
    nkc = qi // (TK // TQ) + 1

    @pl.when(qi == 0)
    def _():
        spread_k = _spread_matrix(N_KV)
        spread_ik = _spread_matrix(2)[0:LANES, 0:2 * LANES]
        lane_v = lax.broadcasted_iota(jnp.int32, (TK, LANES), 1)
        ones = jnp.ones((TK, LANES), BF16)

        def fill(c, carry):
            off = pl.multiple_of(c * TK, TK)
            kp_ref[pl.ds(off, TK), :] = _dot(k_ref[pl.ds(off, TK), :], spread_k).astype(BF16)
            ikp_ref[pl.ds(off, TK), :] = _dot(ik_ref[pl.ds(off, TK), :], spread_ik).astype(BF16)
            for g in range(N_KV):
                src = v_ref[pl.ds(off, TK), LANES * (g // 2):LANES * (g // 2 + 1)]
                keep = (lane_v < 64) if g % 2 == 0 else (lane_v >= 64)
                vaug_ref[pl.ds(off, TK), LANES * g:LANES * (g + 1)] = jnp.where(keep, src, ones)
            return carry

        lax.fori_loop(0, S // TK, fill, 0)

    ident = jnp.where(lax.broadcasted_iota(jnp.int32, (TQ, LANES), 0)
                      == lax.broadcasted_iota(jnp.int32, (TQ, LANES), 1), 1.0, 0.0).astype(BF16)
    for j in range(8):
        iqs_ref[j * TQ:(j + 1) * TQ, :] = iq_ref[:, LANES * j:LANES * (j + 1)]
        qs_ref[j * TQ:(j + 1) * TQ, 0:LANES] = q_ref[:, LANES * j:LANES * (j + 1)] * (HEAD_DIM ** -0.5)
        qs_ref[j * TQ:(j + 1) * TQ, LANES:2 * LANES] = ident

    w = iwt_ref[0:IDX_HEADS, :] * ((IDX_HEADS ** -0.5) * (IDX_DIM ** -0.5))

    keyi = lax.broadcasted_iota(jnp.int32, (TK, TQ), 0)
    qryi = qi * TQ + lax.broadcasted_iota(jnp.int32, (TK, TQ), 1)

    def score_body(c, carry):
        lo, hi = carry
        off = pl.multiple_of(c * TK, TK)
        iqs = iqs_ref[...]
        d_even = _dot_nt(ikp_ref[pl.ds(off, TK), 0:LANES], iqs)
        d_odd = _dot_nt(ikp_ref[pl.ds(off, TK), LANES:2 * LANES], iqs)
        acc = jnp.zeros((TK, TQ), F32)
        for j in range(8):
            acc = acc + jnp.maximum(d_even[:, j * TQ:(j + 1) * TQ], 0.0) * w[2 * j:2 * j + 1, :]
            acc = acc + jnp.maximum(d_odd[:, j * TQ:(j + 1) * TQ], 0.0) * w[2 * j + 1:2 * j + 2, :]
        valid = (off + keyi) <= qryi
        s_ref[pl.ds(off, TK), :] = jnp.where(valid, acc, -jnp.inf)
        hi = jnp.maximum(hi, jnp.max(jnp.where(valid, acc, -jnp.inf), axis=0, keepdims=True))
        lo = jnp.minimum(lo, jnp.min(jnp.where(valid, acc, jnp.inf), axis=0, keepdims=True))
        return lo, hi

    lo0, hi0 = lax.fori_loop(0, nkc, score_body,
                             (jnp.full((1, TQ), jnp.inf, F32), jnp.full((1, TQ), -jnp.inf, F32)))

    def count_ge(mid):
        def body(c, cnt):
            off = pl.multiple_of(c * TK, TK)
            ge = jnp.where(s_ref[pl.ds(off, TK), :] >= mid, 1.0, 0.0)
            return cnt + ge.reshape(TK // CNT_ROWS, CNT_ROWS, TQ).sum(axis=0)

        cnt = lax.fori_loop(0, nkc, body, jnp.zeros((CNT_ROWS, TQ), F32))
        return jnp.sum(cnt, axis=0, keepdims=True)

    @pl.when(qi * TQ + TQ <= TOPK)
    def _():
        thr_ref[...] = jnp.full((8, TQ), jnp.finfo(F32).min, F32)

    @pl.when(qi * TQ + TQ > TOPK)
    def _():
        kf = float(TOPK)

        def step(lo, hi):
            mid = lo + (hi - lo) * 0.5
            cnt = count_ge(mid)
            ge = cnt >= kf
            eq = cnt == kf
            lo2 = jnp.where(ge, mid, lo)
            hi2 = jnp.where(eq, mid, jnp.where(ge, hi, mid))
            stuck = jnp.logical_or(mid <= lo, mid >= hi)
            return lo2, hi2, jnp.logical_and(lo2 < hi2, jnp.logical_not(stuck))

        def cond(st):
            it, _, _, active = st
            return jnp.logical_and(it < BISECT_MAX_IT, active > 0.0)

        def body(st):
            it, lo, hi, _ = st
            lo, hi, _ = step(lo, hi)
            lo, hi, act = step(lo, hi)
            return it + 2, lo, hi, jnp.max(jnp.where(act, 1.0, 0.0))

        _, lo, _, _ = lax.while_loop(cond, body, (jnp.int32(0), lo0, hi0, jnp.float32(1.0)))
        thr_ref[...] = jnp.broadcast_to(lo, (8, TQ))

    m_ref[...] = jnp.full(m_ref.shape, MASKED, F32)
    acc_ref[...] = jnp.zeros(acc_ref.shape, F32)
    thr = thr_ref[0:1, :]
    nt = TK // LANES
    n_blk = 2 * N_KV

    def attn_body(c, carry):
        off = pl.multiple_of(c * TK, TK)
        bias_t = jnp.where(s_ref[pl.ds(off, TK), :] >= thr, 0.0, MASKED).astype(BF16)

        for blk in range(n_blk):
            g, par = blk // 2, blk % 2
            qg = qs_ref[g * 2 * TQ:(g + 1) * 2 * TQ, :]
            kk = kp_ref[pl.ds(off, TK), g * 256 + par * LANES:g * 256 + (par + 1) * LANES]
            lg = _dot_nt(qg, jnp.concatenate([kk, bias_t], axis=1))
            lg_ref[blk] = lg
            m_old = m_ref[blk]
            m_new = jnp.maximum(m_old, jnp.max(lg, axis=1, keepdims=True))
            a_ref[blk] = jnp.exp(m_old - m_new)
            m_ref[blk] = m_new

        for blk in range(n_blk):
            g = blk // 2
            m_new = m_ref[blk]
            p = jnp.concatenate([jnp.exp(lg_ref[blk, :, LANES * t:LANES * (t + 1)] - m_new)
                                 for t in range(nt)], axis=1).astype(BF16)
            va = vaug_ref[pl.ds(off, TK), LANES * g:LANES * (g + 1)]
            acc_ref[blk] = a_ref[blk] * acc_ref[blk] + _dot(p, va)
        return carry

    lax.fori_loop(0, nkc, attn_body, 0)

    lane = lax.broadcasted_iota(jnp.int32, (TQ, LANES), 1)
    for g in range(N_KV):
        acc_e = acc_ref[2 * g]
        acc_o = acc_ref[2 * g + 1]
        o_even = acc_e / pltpu.roll(acc_e, 64, axis=1)
        o_odd = acc_o / pltpu.roll(acc_o, 64, axis=1)
        for pr in range(2):
            xa = o_even[pr * TQ:(pr + 1) * TQ]
            xb = o_odd[pr * TQ:(pr + 1) * TQ]
            if g % 2 == 0:
                left, right = xa, pltpu.roll(xb, 64, axis=1)
            else:
                left, right = pltpu.roll(xa, 64, axis=1), xb
            col = (2 * g + pr) * LANES
            o_ref[:, col:col + LANES] = jnp.where(lane < 64, left, right).astype(o_ref.dtype)


def _attention(a, iwt):
    nq = S // TQ
    rowmap = lambda b, i: b * nq + i
    return pl.pallas_call(
        _attn_kernel,
        grid=(B, nq),
        in_specs=[pl.BlockSpec((TQ, ATTN_W), lambda b, i: (rowmap(b, i), 0)),
                  pl.BlockSpec((TQ, ATTN_W), lambda b, i: (rowmap(b, i), 1)),
                  pl.BlockSpec((LANES, TQ), lambda b, i: (0, rowmap(b, i))),
                  pl.BlockSpec((S, LANES), lambda b, i: (b, 20)),
                  pl.BlockSpec((S, KV_W), lambda b, i: (b, 8)),
                  pl.BlockSpec((S, KV_W), lambda b, i: (b, 9))],
        out_specs=pl.BlockSpec((TQ, ATTN_W), lambda b, i: (rowmap(b, i), 0)),
        out_shape=jax.ShapeDtypeStruct((N, ATTN_W), BF16),
        scratch_shapes=[pltpu.VMEM((S, TQ), F32),
                        pltpu.VMEM((8, TQ), F32),
                        pltpu.VMEM((8 * TQ, 2 * LANES), BF16),
                        pltpu.VMEM((8 * TQ, LANES), BF16),
                        pltpu.VMEM((S, 2 * LANES), BF16),
                        pltpu.VMEM((S, N_KV * 2 * LANES), BF16),
                        pltpu.VMEM((S, N_KV * LANES), BF16),
                        pltpu.VMEM((2 * N_KV, 2 * TQ, LANES), F32),
                        pltpu.VMEM((2 * N_KV, 2 * TQ, LANES), F32),
                        pltpu.VMEM((2 * N_KV, 2 * TQ, LANES), F32),
                        pltpu.VMEM((2 * N_KV, 2 * TQ, TK), F32)],
        compiler_params=_params("arbitrary", "arbitrary"),
        name="sparse_attn",
    )(a, a, iwt, a, a, a)


CONV_TS = 512


def _conv_kernel(bg_ref, cg_ref, xc_ref, cw_ref, o_ref, carry_ref):
    i = pl.program_id(0)

    @pl.when(i % (S // CONV_TS) == 0)
    def _():
        carry_ref[...] = jnp.zeros(carry_ref.shape, F32)

    u = cg_ref[...] * xc_ref[...]
    prev = carry_ref[...]
    row = lax.broadcasted_iota(jnp.int32, u.shape, 0)
    u1 = jnp.where(row == 0, prev[7:8, :], pltpu.roll(u, 1, axis=0))
    u2 = jnp.where(row == 0, prev[6:7, :], jnp.where(row == 1, prev[7:8, :], pltpu.roll(u, 2, axis=0)))
    y = cw_ref[0:1, :] * u2 + cw_ref[1:2, :] * u1 + cw_ref[2:3, :] * u
    o_ref[...] = (bg_ref[...] * y).astype(o_ref.dtype)
    carry_ref[...] = u[CONV_TS - 8:CONV_TS, :]


def _short_conv(pb, conv_w):
    return pl.pallas_call(
        _conv_kernel,
        grid=(N // CONV_TS,),
        in_specs=[pl.BlockSpec((CONV_TS, CONV_W), lambda i: (i, 0)),
                  pl.BlockSpec((CONV_TS, CONV_W), lambda i: (i, 1)),
                  pl.BlockSpec((CONV_TS, CONV_W), lambda i: (i, 2)),
                  pl.BlockSpec((3, CONV_W), lambda i: (0, 0))],
        out_specs=pl.BlockSpec((CONV_TS, CONV_W), lambda i: (i, 0)),
        out_shape=jax.ShapeDtypeStruct((N, CONV_W), BF16),
        scratch_shapes=[pltpu.VMEM((8, CONV_W), F32)],
        compiler_params=_params("arbitrary"),
        name="short_conv",
    )(pb, pb, pb, conv_w)


MG_TM = 1024
MG_TN = 512


def _merge_kernel(oa_ref, zb_ref, wa_ref, wb_ref, ga_ref, gb_ref, o_ref):
    ya = _dot(oa_ref[...], wa_ref[...])
    yb = _dot(zb_ref[...], wb_ref[...])
    merged = jax.nn.sigmoid(ga_ref[...]) * ya + jax.nn.sigmoid(gb_ref[...]) * yb
    o_ref[...] = merged.astype(o_ref.dtype)


def _merge(oa, zb, wa, wb, pb):
    ga0 = (3 * CONV_W) // MG_TN
    gb0 = (3 * CONV_W + D) // MG_TN
    return pl.pallas_call(
        _merge_kernel,
        grid=(N // MG_TM, D // MG_TN),
        in_specs=[pl.BlockSpec((MG_TM, ATTN_W), lambda i, j: (i, 0)),
                  pl.BlockSpec((MG_TM, CONV_W), lambda i, j: (i, 0)),
                  pl.BlockSpec((ATTN_W, MG_TN), lambda i, j: (0, j)),
                  pl.BlockSpec((CONV_W, MG_TN), lambda i, j: (0, j)),
                  pl.BlockSpec((MG_TM, MG_TN), lambda i, j: (i, ga0 + j)),
                  pl.BlockSpec((MG_TM, MG_TN), lambda i, j: (i, gb0 + j))],
        out_specs=pl.BlockSpec((MG_TM, MG_TN), lambda i, j: (i, j)),
        out_shape=jax.ShapeDtypeStruct((N, D), BF16),
        compiler_params=_params("arbitrary", "arbitrary"),
        name="gated_merge",
    )(oa, zb, wa, wb, pb, pb)


def _wo_kernel(m_ref, w_ref, x_ref, mod_ref, o_ref):
    o_ref[...] = x_ref[...] + mod_ref[2:3, :] * _dot(m_ref[...], w_ref[...])


def _out_proj(merged, wo, x2, mod3):
    tpb = S // MG_TM
    return pl.pallas_call(
        _wo_kernel,
        grid=(N // MG_TM, D // MG_TN),
        in_specs=[pl.BlockSpec((MG_TM, D), lambda i, j: (i, 0)),
                  pl.BlockSpec((D, MG_TN), lambda i, j: (0, j)),
                  pl.BlockSpec((MG_TM, MG_TN), lambda i, j: (i, j)),
                  pl.BlockSpec((None, N_MOD, MG_TN), lambda i, j: (i // tpb, 0, j))],
        out_specs=pl.BlockSpec((MG_TM, MG_TN), lambda i, j: (i, j)),
        out_shape=jax.ShapeDtypeStruct((N, D), F32),
        compiler_params=_params("arbitrary", "arbitrary"),
        name="out_proj",
    )(merged, wo, x2, mod3)


RT_TM = 256


def _router_kernel(x_ref, g_ref, mod_ref, wr_ref, br_ref,
                   h_ref, mi_ref, mw_ref, cnt_ref, carry_ref):
    i = pl.program_id(0)

    @pl.when(i == 0)
    def _():
        carry_ref[...] = jnp.zeros(carry_ref.shape, F32)

    h = _modulated_norm(x_ref[...], g_ref[...], mod_ref[3:4, :], mod_ref[4:5, :])
    h_ref[...] = h
    lg = _dot(h.astype(BF16), wr_ref[...]) + br_ref[...]
    lane = lax.broadcasted_iota(jnp.int32, lg.shape, 1)
    ninf = -jnp.inf

    gl = jnp.where(lane < N_GROUPS, lg, ninf)
    gm = jnp.max(gl, axis=1, keepdims=True)
    g_sel = jnp.min(jnp.where(gl == gm, lane, LANES), axis=1, keepdims=True)
    p_g = 1.0 / jnp.sum(jnp.exp(gl - gm), axis=1, keepdims=True)

    grp = jnp.right_shift(lane - N_GROUPS, 3)
    in_grp = jnp.logical_and(jnp.logical_and(lane >= N_GROUPS, lane < N_GROUPS + N_EXP), grp == g_sel)
    el = jnp.where(in_grp, lg, ninf)
    m1 = jnp.max(el, axis=1, keepdims=True)
    i1 = jnp.min(jnp.where(el == m1, lane, LANES), axis=1, keepdims=True)
    el2 = jnp.where(lane == i1, ninf, el)
    m2 = jnp.max(el2, axis=1, keepdims=True)
    i2 = jnp.min(jnp.where(el2 == m2, lane, LANES), axis=1, keepdims=True)
    e = jnp.exp(m2 - m1)
    w1 = p_g * (1.0 / (1.0 + e))
    w2 = p_g * (e / (1.0 + e))
    e1 = i1 - N_GROUPS
    e2 = i2 - N_GROUPS

    oh1 = lane == e1
    oh2 = lane == e2
    used = jnp.where(jnp.logical_or(oh1, oh2), 1.0, 0.0)
    ri = lax.broadcasted_iota(jnp.int32, (RT_TM, RT_TM), 0)
    ci = lax.broadcasted_iota(jnp.int32, (RT_TM, RT_TM), 1)
    lower = jnp.where(ci < ri, 1.0, 0.0).astype(BF16)
    before = _dot(lower, used.astype(BF16)) + carry_ref[...]
    r1 = jnp.sum(jnp.where(oh1, before, 0.0), axis=1, keepdims=True).astype(jnp.int32)
    r2 = jnp.sum(jnp.where(oh2, before, 0.0), axis=1, keepdims=True).astype(jnp.int32)
    total = carry_ref[...] + jnp.sum(used, axis=0, keepdims=True)
    carry_ref[...] = total
    cnt_ref[...] = total

    zero = jnp.zeros_like(lane)
    mi_ref[...] = jnp.where(lane == 0, e1, jnp.where(lane == 1, e2,
                            jnp.where(lane == 2, r1, jnp.where(lane == 3, r2, zero))))
    mw_ref[...] = jnp.where(lane == 0, w1, jnp.where(lane == 1, w2, 0.0))


def _router(x1, g, mod3, wr, br):
    tpb = S // RT_TM
    return pl.pallas_call(
        _router_kernel,
        grid=(N // RT_TM,),
        in_specs=[pl.BlockSpec((RT_TM, D), lambda i: (i, 0)),
                  pl.BlockSpec((1, D), lambda i: (0, 0)),
                  pl.BlockSpec((None, N_MOD, D), lambda i: (i // tpb, 0, 0)),
                  pl.BlockSpec((D, LANES), lambda i: (0, 0)),
                  pl.BlockSpec((1, LANES), lambda i: (0, 0))],
        out_specs=[pl.BlockSpec((RT_TM, D), lambda i: (i, 0)),
                   pl.BlockSpec((RT_TM, LANES), lambda i: (i, 0)),
                   pl.BlockSpec((RT_TM, LANES), lambda i: (i, 0)),
                   pl.BlockSpec((1, LANES), lambda i: (0, 0))],
        out_shape=[jax.ShapeDtypeStruct((N, D), F32),
                   jax.ShapeDtypeStruct((N, LANES), jnp.int32),
                   jax.ShapeDtypeStruct((N, LANES), F32),
                   jax.ShapeDtypeStruct((1, LANES), F32)],
        scratch_shapes=[pltpu.VMEM((1, LANES), F32)],
        compiler_params=_params("arbitrary"),
        name="norm2_router",
    )(x1, g.reshape(1, D), mod3, wr, br)


DP_TM = 256


def _row_copy(src_ref, src_row, dst_ref, dst_row, sem):
    return pltpu.make_async_copy(src_ref.at[pl.ds(src_row, 1), :], dst_ref.at[pl.ds(dst_row, 1), :], sem)


def _dispatch_kernel(slot_ref, h_ref, xs_in_ref, xs_ref, sem):
    del xs_in_ref
    base = pl.program_id(0) * DP_TM

    def issue(r, carry):
        for k in range(2):
            _row_copy(h_ref, r, xs_ref, slot_ref[(base + r) * 2 + k], sem).start()
        return carry

    lax.fori_loop(0, DP_TM, issue, 0, unroll=8)

    def drain(r, carry):
        for _ in range(2):
            _row_copy(h_ref, 0, xs_ref, 0, sem).wait()
        return carry

    lax.fori_loop(0, DP_TM, drain, 0, unroll=8)


def _dispatch(slot, h2, xs0):
    grid_spec = pltpu.PrefetchScalarGridSpec(
        num_scalar_prefetch=1,
        grid=(N // DP_TM,),
        in_specs=[pl.BlockSpec((DP_TM, D), lambda i, *_: (i, 0)),
                  pl.BlockSpec(memory_space=pl.ANY)],
        out_specs=pl.BlockSpec(memory_space=pl.ANY),
        scratch_shapes=[pltpu.SemaphoreType.DMA(())],
    )
    return pl.pallas_call(
        _dispatch_kernel,
        grid_spec=grid_spec,
        out_shape=jax.ShapeDtypeStruct((XS_ROWS, D), F32),
        input_output_aliases={2: 0},
        compiler_params=_params("arbitrary"),
        name="moe_dispatch",
    )(slot, h2, xs0)


def _expert_kernel(ce_ref, nc_ref, x_ref, wg_ref, wu_ref, wd_ref, y_ref, wg_s, wu_s, wd_s):
    c = pl.program_id(0)
    valid = c < nc_ref[0]
    new_expert = jnp.logical_or(c == 0, ce_ref[c] != ce_ref[jnp.maximum(c - 1, 0)])

    @pl.when(jnp.logical_and(valid, new_expert))
    def _():
        wg_s[...] = wg_ref[...].astype(BF16)
        wu_s[...] = wu_ref[...].astype(BF16)
        wd_s[...] = wd_ref[...].astype(BF16)

    @pl.when(valid)
    def _():
        x = x_ref[...].astype(BF16)
        a = _dot(x, wg_s[...])
        u = _dot(x, wu_s[...])
        act = (a * jax.nn.sigmoid(a)) * u
        y_ref[...] = _dot(act.astype(BF16), wd_s[...])


def _experts(ce, nc, xs, wg, wu, wd):
    rows = lambda c, ce, nc: (jnp.minimum(c, nc[0] - 1), 0)
    wmap = lambda c, ce, nc: (ce[c], 0, 0)
    grid_spec = pltpu.PrefetchScalarGridSpec(
        num_scalar_prefetch=2,
        grid=(MAX_CHUNKS,),
        in_specs=[pl.BlockSpec((CH, D), rows),
                  pl.BlockSpec((None, D, FF), wmap),
                  pl.BlockSpec((None, D, FF), wmap),
                  pl.BlockSpec((None, FF, D), wmap)],
        out_specs=pl.BlockSpec((CH, D), rows),
        scratch_shapes=[pltpu.VMEM((D, FF), BF16), pltpu.VMEM((D, FF), BF16), pltpu.VMEM((FF, D), BF16)],
    )
    return pl.pallas_call(
        _expert_kernel,
        grid_spec=grid_spec,
        out_shape=jax.ShapeDtypeStruct((XS_ROWS, D), F32),
        input_output_aliases={2: 0},
        compiler_params=_params("arbitrary"),
        name="moe_experts",
    )(ce, nc, xs, wg, wu, wd)


CB_TM = 256


def _combine_kernel(slot_ref, y_ref, x_ref, mw_ref, mod_ref, g_ref, o_ref, ybuf, sem):
    i = pl.program_id(0)
    cur = i % 2

    def issue(tile, buf):
        def body(r, carry):
            for k in range(2):
                _row_copy(y_ref, slot_ref[(tile * CB_TM + r) * 2 + k], ybuf.at[buf, k], r, sem.at[buf]).start()
            return carry

        lax.fori_loop(0, CB_TM, body, 0, unroll=8)

    @pl.when(i == 0)
    def _():
        issue(0, 0)

    @pl.when(i + 1 < pl.num_programs(0))
    def _():
        issue(i + 1, 1 - cur)

    def drain(r, carry):
        for k in range(2):
            _row_copy(y_ref, 0, ybuf.at[cur, k], 0, sem.at[cur]).wait()
        return carry

    lax.fori_loop(0, CB_TM, drain, 0, unroll=8)

    moe = mw_ref[:, 0:1] * ybuf[cur, 0] + mw_ref[:, 1:2] * ybuf[cur, 1]
    x = x_ref[...] + mod_ref[5:6, :] * moe
    r = lax.rsqrt(jnp.mean(x * x, axis=-1, keepdims=True) + EPS)
    o_ref[...] = (x * r) * g_ref[...]


def _combine(slot, y, x1, mw, mod3, g):
    tpb = S // CB_TM
    grid_spec = pltpu.PrefetchScalarGridSpec(
        num_scalar_prefetch=1,
        grid=(N // CB_TM,),
        in_specs=[pl.BlockSpec(memory_space=pl.ANY),
                  pl.BlockSpec((CB_TM, D), lambda i, *_: (i, 0)),
                  pl.BlockSpec((CB_TM, LANES), lambda i, *_: (i, 0)),
                  pl.BlockSpec((None, N_MOD, D), lambda i, *_: (i // tpb, 0, 0)),
                  pl.BlockSpec((1, D), lambda i, *_: (0, 0))],
        out_specs=pl.BlockSpec((CB_TM, D), lambda i, *_: (i, 0)),
        scratch_shapes=[pltpu.VMEM((2, 2, CB_TM, D), F32), pltpu.SemaphoreType.DMA((2,))],
    )
    return pl.pallas_call(
        _combine_kernel,
        grid_spec=grid_spec,
        out_shape=jax.ShapeDtypeStruct((N, D), F32),
        compiler_params=_params("arbitrary"),
        name="moe_combine_norm",
    )(slot, y, x1, mw, mod3, g.reshape(1, D))


def kernel(x, c, w_ada, b_ada, g_norm_mix, w_in, conv_w, w_a_up, w_b_out, w_o, g_norm_ffn,
           w_rg, b_rg, w_re, b_re, w_e_gate, w_e_up, w_e_down, g_norm_final):
    x2 = x.reshape(N, D)
    mod3 = _ada(c, w_ada[0], b_ada[0]).reshape(B, N_MOD, D)

    wi = w_in[0]
    o_q, o_k, o_v, o_iq, o_ik, o_iw, o_bg = 0, 1024, 1280, 1536, 2560, 2624, 2640
    w_a = jnp.concatenate([wi[:, o_q:o_k], wi[:, o_iq:o_ik], wi[:, o_k:o_iq], wi[:, o_ik:o_iw],
                           jnp.zeros((D, LANES - IDX_DIM), F32)], axis=1).astype(BF16)
    w_iw_t = jnp.pad(wi[:, o_iw:o_bg].T, ((0, LANES - IDX_HEADS), (0, 0))).astype(BF16)
    w_b = wi[:, o_bg:].astype(BF16)

    h, pa, iwt = _norm_proj(x2, g_norm_mix[0], mod3, w_a, w_iw_t)
    pb = _matmul(h, w_b, F32, 1024, 512, "proj_conv_gates")

    oa = _attention(pa, iwt)
    zb = _short_conv(pb, conv_w[0])
    merged = _merge(oa, zb, w_a_up[0].astype(BF16), w_b_out[0].astype(BF16), pb)
    x1 = _out_proj(merged, w_o[0].astype(BF16), x2, mod3)

    wr = jnp.pad(jnp.concatenate([w_rg[0], w_re[0]], axis=1),
                 ((0, 0), (0, LANES - N_GROUPS - N_EXP))).astype(BF16)
    br = jnp.pad(jnp.concatenate([b_rg[0], b_re[0]]), (0, LANES - N_GROUPS - N_EXP)).reshape(1, LANES)
    h2, mi, mw, cnt = _router(x1, g_norm_ffn[0], mod3, wr, br)

    counts = cnt[0, :N_EXP].astype(jnp.int32)
    nch = (counts + CH - 1) // CH
    cum = jnp.cumsum(nch)
    offs = ((cum - nch) * CH).astype(jnp.int32)
    total = cum[-1]
    cidx = jnp.minimum(jnp.arange(MAX_CHUNKS, dtype=jnp.int32), total - 1)
    ce = jnp.minimum(jnp.sum((cum[None, :] <= cidx[:, None]).astype(jnp.int32), axis=1), N_EXP - 1)
    nc = total.reshape(1).astype(jnp.int32)
    seg = jnp.sum(jnp.where(mi[:, 0:2, None] == jnp.arange(N_EXP, dtype=jnp.int32), offs, 0), axis=-1)
    slot = (seg + mi[:, 2:4]).reshape(-1)

    xs = _dispatch(slot, h2, jnp.zeros((XS_ROWS, D), F32))
    y = _experts(ce, nc, xs, w_e_gate[0], w_e_up[0], w_e_down[0])
    out = _combine(slot, y, x1, mw, mod3, g_norm_final)
    return out.reshape(B, S, D)
```

```python
import functools

import jax
import jax.numpy as jnp
from jax import lax
from jax.experimental import pallas as pl
from jax.experimental.pallas import tpu as pltpu

F32 = jnp.float32
BF16 = jnp.bfloat16

D = 2048
B = 2
S = 4096
N = B * S
EPS = 1e-6
HEAD_DIM = 64
ATTN_W = 1024
N_KV = 4
KV_W = 256
IDX_HEADS = 16
IDX_DIM = 64
TOPK = min(256, S // 4)
CONV_W = 1024
N_GROUPS = 4
EPG = 8
N_EXP = 32
FF = 512
N_MOD = 6

LANES = 128
TQ = 128
TK = 512
CH = 576
MAX_CHUNKS = (N * 2) // CH + N_EXP
XS_ROWS = MAX_CHUNKS * CH
MASKED = -1e30
BISECT_MAX_IT = 320
TIE_STEPS = 13
CNT_ROWS = 64
VMEM_LIMIT = 56 * 1024 * 1024


def _dot(a, b):
    return jnp.dot(a, b, preferred_element_type=F32)


def _dot_nt(a, b):
    return lax.dot_general(a, b, (((1,), (1,)), ((), ())), preferred_element_type=F32)


def _params(*sem):
    return pltpu.CompilerParams(dimension_semantics=sem, vmem_limit_bytes=VMEM_LIMIT)


ADA_TN = 1024
ADA_KC = 256


def _ada_kernel(ct_ref, w_ref, b_ref, o_ref):
    tn = w_ref.shape[1]

    def body(i, accs):
        k0 = pl.multiple_of(i * ADA_KC, ADA_KC)
        ct = ct_ref[pl.ds(k0, ADA_KC), :]
        act = ct * jax.nn.sigmoid(ct)
        w = w_ref[pl.ds(k0, ADA_KC), :]
        out = []
        for b in range(B):
            prod = w * act[:, b:b + 1]
            out.append(accs[b] + prod.reshape(ADA_KC // 8, 8, tn).sum(axis=0))
        return tuple(out)

    accs = lax.fori_loop(0, D // ADA_KC, body, tuple(jnp.zeros((8, tn), F32) for _ in range(B)))
    rows = [jnp.sum(a, axis=0, keepdims=True) for a in accs]
    o_ref[...] = jnp.concatenate(rows, axis=0) + b_ref[...]


def _ada(c, w, b):
    n_out = w.shape[1]
    return pl.pallas_call(
        _ada_kernel,
        grid=(n_out // ADA_TN,),
        in_specs=[pl.BlockSpec((D, B), lambda j: (0, 0)),
                  pl.BlockSpec((D, ADA_TN), lambda j: (0, j)),
                  pl.BlockSpec((1, ADA_TN), lambda j: (0, j))],
        out_specs=pl.BlockSpec((B, ADA_TN), lambda j: (0, j)),
        out_shape=jax.ShapeDtypeStruct((B, n_out), F32),
        compiler_params=_params("arbitrary"),
        name="ada_mod",
    )(c.T, w, b.reshape(1, n_out))


NORM_TM = 512


def _modulated_norm(x, g, shift, scale):
    r = lax.rsqrt(jnp.mean(x * x, axis=-1, keepdims=True) + EPS)
    return (x * r) * g * (1.0 + scale) + shift


def _norm_proj_kernel(x_ref, g_ref, mod_ref, wa_ref, wiw_ref, h_ref, pa_ref, iwt_ref):
    h = _modulated_norm(x_ref[...], g_ref[...], mod_ref[0:1, :], mod_ref[1:2, :]).astype(BF16)
    h_ref[...] = h
    pa_ref[...] = _dot(h, wa_ref[...]).astype(pa_ref.dtype)
    iwt_ref[...] = _dot_nt(wiw_ref[...], h)


def _norm_proj(x2, g, mod3, w_a, w_iw_t):
    tpb = S // NORM_TM
    n_a = w_a.shape[1]
    return pl.pallas_call(
        _norm_proj_kernel,
        grid=(N // NORM_TM,),
        in_specs=[pl.BlockSpec((NORM_TM, D), lambda i: (i, 0)),
                  pl.BlockSpec((1, D), lambda i: (0, 0)),
                  pl.BlockSpec((None, N_MOD, D), lambda i: (i // tpb, 0, 0)),
                  pl.BlockSpec((D, n_a), lambda i: (0, 0)),
                  pl.BlockSpec((LANES, D), lambda i: (0, 0))],
        out_specs=[pl.BlockSpec((NORM_TM, D), lambda i: (i, 0)),
                   pl.BlockSpec((NORM_TM, n_a), lambda i: (i, 0)),
                   pl.BlockSpec((LANES, NORM_TM), lambda i: (0, i))],
        out_shape=[jax.ShapeDtypeStruct((N, D), BF16),
                   jax.ShapeDtypeStruct((N, n_a), BF16),
                   jax.ShapeDtypeStruct((LANES, N), F32)],
        compiler_params=_params("arbitrary"),
        name="norm1_proj_attn",
    )(x2, g.reshape(1, D), mod3, w_a, w_iw_t)


def _mm_kernel(a_ref, w_ref, o_ref):
    o_ref[...] = _dot(a_ref[...], w_ref[...]).astype(o_ref.dtype)


def _matmul(a, w, out_dtype, tm, tn, name):
    m, k = a.shape
    n = w.shape[1]
    return pl.pallas_call(
        _mm_kernel,
        grid=(m // tm, n // tn),
        in_specs=[pl.BlockSpec((tm, k), lambda i, j: (i, 0)),
                  pl.BlockSpec((k, tn), lambda i, j: (0, j))],
        out_specs=pl.BlockSpec((tm, tn), lambda i, j: (i, j)),
        out_shape=jax.ShapeDtypeStruct((m, n), out_dtype),
        compiler_params=_params("arbitrary", "arbitrary"),
        name=name,
    )(a, w)


A_COLS = 2 * ATTN_W + 2 * KV_W + LANES


def _spread_matrix(groups):
    r = lax.broadcasted_iota(jnp.int32, (groups * HEAD_DIM, groups * 4 * HEAD_DIM), 0)
    c = lax.broadcasted_iota(jnp.int32, (groups * HEAD_DIM, groups * 4 * HEAD_DIM), 1)
    d, cc = r & (HEAD_DIM - 1), c & (4 * HEAD_DIM - 1)
    hit = jnp.logical_and(jnp.right_shift(r, 6) == jnp.right_shift(c, 8),
                          jnp.logical_or(cc == d, cc == d + 3 * HEAD_DIM))
    return jnp.where(hit, 1.0, 0.0).astype(BF16)


def _attn_kernel(q_ref, iq_ref, iwt_ref, ik_ref, k_ref, v_ref, o_ref,
                 s_ref, thr_ref, tie_ref, qs_ref, iqs_ref, ikp_ref, kp_ref, vaug_ref, acc_ref,
                 lg0_ref, a0_ref, m0_ref, lg1_ref, a1_ref, m1_ref):
    qi = pl.program_id(1)
    nkc = qi // (TK // TQ) + 1

    @pl.when(qi == 0)
    def _():
        spread_k = _spread_matrix(N_KV)
        spread_ik = _spread_matrix(2)[0:LANES, 0:2 * LANES]
        lane_v = lax.broadcasted_iota(jnp.int32, (TK, LANES), 1)
        ones = jnp.ones((TK, LANES), BF16)

        def fill(c, carry):
            off = pl.multiple_of(c * TK, TK)
            kp_ref[pl.ds(off, TK), :] = _dot(k_ref[pl.ds(off, TK), :], spread_k).astype(BF16)
            ikp_ref[pl.ds(off, TK), :] = _dot(ik_ref[pl.ds(off, TK), :], spread_ik).astype(BF16)
            for g in range(N_KV):
                src = v_ref[pl.ds(off, TK), LANES * (g // 2):LANES * (g // 2 + 1)]
                keep = (lane_v < 64) if g % 2 == 0 else (lane_v >= 64)
                vaug_ref[pl.ds(off, TK), LANES * g:LANES * (g + 1)] = jnp.where(keep, src, ones)
            return carry

        lax.fori_loop(0, S // TK, fill, 0)

    ident = jnp.where(lax.broadcasted_iota(jnp.int32, (TQ, LANES), 0)
                      == lax.broadcasted_iota(jnp.int32, (TQ, LANES), 1), 1.0, 0.0).astype(BF16)
    for j in range(8):
        iqs_ref[j * TQ:(j + 1) * TQ, :] = iq_ref[:, LANES * j:LANES * (j + 1)]
        qs_ref[j * TQ:(j + 1) * TQ, 0:LANES] = q_ref[:, LANES * j:LANES * (j + 1)] * (HEAD_DIM ** -0.5)
        qs_ref[j * TQ:(j + 1) * TQ, LANES:2 * LANES] = ident

    w = iwt_ref[0:IDX_HEADS, :] * ((IDX_HEADS ** -0.5) * (IDX_DIM ** -0.5))

    keyi = lax.broadcasted_iota(jnp.int32, (TK, TQ), 0)
    qryi = qi * TQ + lax.broadcasted_iota(jnp.int32, (TK, TQ), 1)

    def score_body(c, carry):
        lo, hi = carry
        off = pl.multiple_of(c * TK, TK)
        ik_even = ikp_ref[pl.ds(off, TK), 0:LANES]
        ik_odd = ikp_ref[pl.ds(off, TK), LANES:2 * LANES]
        acc = jnp.zeros((TK, TQ), F32)
        for jj in range(4):
            iq2 = iqs_ref[jj * 2 * TQ:(jj + 1) * 2 * TQ, :]
            for par, ik_par in ((0, ik_even), (1, ik_odd)):
                d = _dot_nt(ik_par, iq2)
                for t in range(2):
                    h = 2 * (2 * jj + t) + par
                    acc = acc + jnp.maximum(d[:, t * TQ:(t + 1) * TQ], 0.0) * w[h:h + 1, :]
        valid = (off + keyi) <= qryi
        s_ref[pl.ds(off, TK), :] = jnp.where(valid, acc, -jnp.inf)
        hi = jnp.maximum(hi, jnp.max(jnp.where(valid, acc, -jnp.inf), axis=0, keepdims=True))
        lo = jnp.minimum(lo, jnp.min(jnp.where(valid, acc, jnp.inf), axis=0, keepdims=True))
        return lo, hi

    lo0, hi0 = lax.fori_loop(0, nkc, score_body,
                             (jnp.full((1, TQ), jnp.inf, F32), jnp.full((1, TQ), -jnp.inf, F32)))

    def count_keys(pred):
        def body(c, cnt):
            off = pl.multiple_of(c * TK, TK)
            hit = jnp.where(pred(s_ref[pl.ds(off, TK), :], off + keyi), 1.0, 0.0)
            return cnt + hit.reshape(TK // CNT_ROWS, CNT_ROWS, TQ).sum(axis=0)

        cnt = lax.fori_loop(0, nkc, body, jnp.zeros((CNT_ROWS, TQ), F32))
        return jnp.sum(cnt, axis=0, keepdims=True)

    def count_ge(mid):
        return count_keys(lambda sc, _: sc >= mid)

    tie_ref[...] = jnp.full((8, TQ), S, jnp.int32)

    @pl.when(qi * TQ + TQ <= TOPK)
    def _():
        thr_ref[...] = jnp.full((8, TQ), jnp.finfo(F32).min, F32)

    @pl.when(qi * TQ + TQ > TOPK)
    def _():
        kf = float(TOPK)

        def step(lo, hi):
            mid = lo + (hi - lo) * 0.5
            cnt = count_ge(mid)
            ge = cnt >= kf
            eq = cnt == kf
            lo2 = jnp.where(ge, mid, lo)
            hi2 = jnp.where(eq, mid, jnp.where(ge, hi, mid))
            stuck = jnp.logical_or(mid <= lo, mid >= hi)
            return lo2, hi2, jnp.logical_and(lo2 < hi2, jnp.logical_not(stuck))

        def cond(st):
            it, _, _, active = st
            return jnp.logical_and(it < BISECT_MAX_IT, active > 0.0)

        def body(st):
            it, lo, hi, _ = st
            lo, hi, _ = step(lo, hi)
            lo, hi, act = step(lo, hi)
            return it + 2, lo, hi, jnp.max(jnp.where(act, 1.0, 0.0))

        _, lo, _, _ = lax.while_loop(cond, body, (jnp.int32(0), lo0, hi0, jnp.float32(1.0)))
        thr_ref[...] = jnp.broadcast_to(lo, (8, TQ))

        need = count_ge(lo) > kf

        @pl.when(jnp.max(jnp.where(need, 1.0, 0.0)) > 0.0)
        def _():
            keep = kf - count_keys(lambda sc, _: sc > lo)

            def index_step(_, st):
                below, upto = st
                mid = jnp.right_shift(below + upto, 1)
                ok = count_keys(lambda sc, ki: jnp.logical_and(sc == lo, ki <= mid)) >= keep
                return jnp.where(ok, below, mid), jnp.where(ok, mid, upto)

            _, upto = lax.fori_loop(0, TIE_STEPS, index_step,
                                    (jnp.full((1, TQ), -1, jnp.int32), jnp.full((1, TQ), S - 1, jnp.int32)))
            tie_ref[...] = jnp.broadcast_to(jnp.where(need, upto, S), (8, TQ))

    acc_ref[...] = jnp.zeros(acc_ref.shape, F32)
    thr = thr_ref[0:1, :]
    tie = tie_ref[0:1, :]
    nt = TK // LANES
    n_blk = 2 * N_KV
    bufs = ((lg0_ref, a0_ref, m0_ref), (lg1_ref, a1_ref, m1_ref))
    m1_ref[...] = jnp.full(m1_ref.shape, MASKED, F32)

    def pass_a(c, slot):
        lg_ref, a_ref, m_ref = bufs[slot]
        m_prev_ref = bufs[1 - slot][2]
        off = pl.multiple_of(c * TK, TK)
        sc = s_ref[pl.ds(off, TK), :]
        sel = jnp.logical_or(sc > thr, jnp.logical_and(sc == thr, (off + keyi) <= tie))
        bias_t = jnp.where(sel, 0.0, MASKED).astype(BF16)
        for blk in range(n_blk):
            g, par = blk // 2, blk % 2
            qg = qs_ref[g * 2 * TQ:(g + 1) * 2 * TQ, :]
            kk = kp_ref[pl.ds(off, TK), g * 256 + par * LANES:g * 256 + (par + 1) * LANES]
            lg = _dot_nt(qg, jnp.concatenate([kk, bias_t], axis=1))
            lg_ref[blk] = lg
            m_old = m_prev_ref[blk]
            m_new = jnp.maximum(m_old, jnp.max(lg, axis=1, keepdims=True))
            a_ref[blk] = jnp.exp(m_old - m_new)
            m_ref[blk] = m_new

    def pass_b(c, slot):
        lg_ref, a_ref, m_ref = bufs[slot]
        off = pl.multiple_of(c * TK, TK)
        for blk in range(n_blk):
            g = blk // 2
            m_new = m_ref[blk]
            p = jnp.concatenate([jnp.exp(lg_ref[blk, :, LANES * t:LANES * (t + 1)] - m_new)
                                 for t in range(nt)], axis=1).astype(BF16)
            va = vaug_ref[pl.ds(off, TK), LANES * g:LANES * (g + 1)]
            acc_ref[blk] = a_ref[blk] * acc_ref[blk] + _dot(p, va)

    pass_a(0, 0)

    def pair_body(i, carry):
        c = 2 * i
        pass_a(c + 1, 1)
        pass_b(c, 0)
        pass_a(c + 2, 0)
        pass_b(c + 1, 1)
        return carry

    lax.fori_loop(0, (nkc - 1) // 2, pair_body, 0)

    @pl.when(nkc % 2 == 0)
    def _():
        pass_a(nkc - 1, 1)
        pass_b(nkc - 2, 0)
        pass_b(nkc - 1, 1)

    @pl.when(nkc % 2 == 1)
    def _():
        pass_b(nkc - 1, 0)

    lane = lax.broadcasted_iota(jnp.int32, (TQ, LANES), 1)
    for g in range(N_KV):
        acc_e = acc_ref[2 * g]
        acc_o = acc_ref[2 * g + 1]
        o_even = acc_e / pltpu.roll(acc_e, 64, axis=1)
        o_odd = acc_o / pltpu.roll(acc_o, 64, axis=1)
        for pr in range(2):
            xa = o_even[pr * TQ:(pr + 1) * TQ]
            xb = o_odd[pr * TQ:(pr + 1) * TQ]
            if g % 2 == 0:
                left, right = xa, pltpu.roll(xb, 64, axis=1)
            else:
                left, right = pltpu.roll(xa, 64, axis=1), xb
            col = (2 * g + pr) * LANES
            o_ref[:, col:col + LANES] = jnp.where(lane < 64, left, right).astype(o_ref.dtype)


def _attention(a, iwt):
    nq = S // TQ
    rowmap = lambda b, i: b * nq + i
    return pl.pallas_call(
        _attn_kernel,
        grid=(B, nq),
        in_specs=[pl.BlockSpec((TQ, ATTN_W), lambda b, i: (rowmap(b, i), 0)),
                  pl.BlockSpec((TQ, ATTN_W), lambda b, i: (rowmap(b, i), 1)),
                  pl.BlockSpec((LANES, TQ), lambda b, i: (0, rowmap(b, i))),
                  pl.BlockSpec((S, LANES), lambda b, i: (b, 20)),
                  pl.BlockSpec((S, KV_W), lambda b, i: (b, 8)),
                  pl.BlockSpec((S, KV_W), lambda b, i: (b, 9))],
        out_specs=pl.BlockSpec((TQ, ATTN_W), lambda b, i: (rowmap(b, i), 0)),
        out_shape=jax.ShapeDtypeStruct((N, ATTN_W), BF16),
        scratch_shapes=[pltpu.VMEM((S, TQ), F32),
                        pltpu.VMEM((8, TQ), F32),
                        pltpu.VMEM((8, TQ), jnp.int32),
                        pltpu.VMEM((8 * TQ, 2 * LANES), BF16),
                        pltpu.VMEM((8 * TQ, LANES), BF16),
                        pltpu.VMEM((S, 2 * LANES), BF16),
                        pltpu.VMEM((S, N_KV * 2 * LANES), BF16),
                        pltpu.VMEM((S, N_KV * LANES), BF16),
                        pltpu.VMEM((2 * N_KV, 2 * TQ, LANES), F32)]
                       + 2 * [pltpu.VMEM((2 * N_KV, 2 * TQ, TK), F32),
                              pltpu.VMEM((2 * N_KV, 2 * TQ, LANES), F32),
                              pltpu.VMEM((2 * N_KV, 2 * TQ, LANES), F32)],

        compiler_params=_params("arbitrary", "arbitrary"),
        name="sparse_attn",
    )(a, a, iwt, a, a, a)


CONV_TS = 512


def _conv_kernel(bg_ref, cg_ref, xc_ref, cw_ref, o_ref, carry_ref):
    i = pl.program_id(0)

    @pl.when(i % (S // CONV_TS) == 0)
    def _():
        carry_ref[...] = jnp.zeros(carry_ref.shape, F32)

    u = cg_ref[...] * xc_ref[...]
    prev = carry_ref[...]
    row = lax.broadcasted_iota(jnp.int32, u.shape, 0)
    u1 = jnp.where(row == 0, prev[7:8, :], pltpu.roll(u, 1, axis=0))
    u2 = jnp.where(row == 0, prev[6:7, :], jnp.where(row == 1, prev[7:8, :], pltpu.roll(u, 2, axis=0)))
    y = cw_ref[0:1, :] * u2 + cw_ref[1:2, :] * u1 + cw_ref[2:3, :] * u
    o_ref[...] = (bg_ref[...] * y).astype(o_ref.dtype)
    carry_ref[...] = u[CONV_TS - 8:CONV_TS, :]


def _short_conv(pb, conv_w):
    return pl.pallas_call(
        _conv_kernel,
        grid=(N // CONV_TS,),
        in_specs=[pl.BlockSpec((CONV_TS, CONV_W), lambda i: (i, 0)),
                  pl.BlockSpec((CONV_TS, CONV_W), lambda i: (i, 1)),
                  pl.BlockSpec((CONV_TS, CONV_W), lambda i: (i, 2)),
                  pl.BlockSpec((3, CONV_W), lambda i: (0, 0))],
        out_specs=pl.BlockSpec((CONV_TS, CONV_W), lambda i: (i, 0)),
        out_shape=jax.ShapeDtypeStruct((N, CONV_W), BF16),
        scratch_shapes=[pltpu.VMEM((8, CONV_W), F32)],
        compiler_params=_params("arbitrary"),
        name="short_conv",
    )(pb, pb, pb, conv_w)


MG_TM = 1024
MG_TN = 512


def _merge_kernel(oa_ref, zb_ref, wa_ref, wb_ref, ga_ref, gb_ref, o_ref):
    ya = _dot(oa_ref[...], wa_ref[...])
    yb = _dot(zb_ref[...], wb_ref[...])
    merged = jax.nn.sigmoid(ga_ref[...]) * ya + jax.nn.sigmoid(gb_ref[...]) * yb
    o_ref[...] = merged.astype(o_ref.dtype)


def _merge(oa, zb, wa, wb, pb):
    ga0 = (3 * CONV_W) // MG_TN
    gb0 = (3 * CONV_W + D) // MG_TN
    return pl.pallas_call(
        _merge_kernel,
        grid=(N // MG_TM, D // MG_TN),
        in_specs=[pl.BlockSpec((MG_TM, ATTN_W), lambda i, j: (i, 0)),
                  pl.BlockSpec((MG_TM, CONV_W), lambda i, j: (i, 0)),
                  pl.BlockSpec((ATTN_W, MG_TN), lambda i, j: (0, j)),
                  pl.BlockSpec((CONV_W, MG_TN), lambda i, j: (0, j)),
                  pl.BlockSpec((MG_TM, MG_TN), lambda i, j: (i, ga0 + j)),
                  pl.BlockSpec((MG_TM, MG_TN), lambda i, j: (i, gb0 + j))],
        out_specs=pl.BlockSpec((MG_TM, MG_TN), lambda i, j: (i, j)),
        out_shape=jax.ShapeDtypeStruct((N, D), BF16),
        compiler_params=_params("arbitrary", "arbitrary"),
        name="gated_merge",
    )(oa, zb, wa, wb, pb, pb)


def _wo_kernel(m_ref, w_ref, x_ref, mod_ref, o_ref):
    o_ref[...] = x_ref[...] + mod_ref[2:3, :] * _dot(m_ref[...], w_ref[...])


def _out_proj(merged, wo, x2, mod3):
    tpb = S // MG_TM
    return pl.pallas_call(
        _wo_kernel,
        grid=(N // MG_TM, D // MG_TN),
        in_specs=[pl.BlockSpec((MG_TM, D), lambda i, j: (i, 0)),
                  pl.BlockSpec((D, MG_TN), lambda i, j: (0, j)),
                  pl.BlockSpec((MG_TM, MG_TN), lambda i, j: (i, j)),
                  pl.BlockSpec((None, N_MOD, MG_TN), lambda i, j: (i // tpb, 0, j))],
        out_specs=pl.BlockSpec((MG_TM, MG_TN), lambda i, j: (i, j)),
        out_shape=jax.ShapeDtypeStruct((N, D), F32),
        compiler_params=_params("arbitrary", "arbitrary"),
        name="out_proj",
    )(merged, wo, x2, mod3)


RT_TM = 256


def _router_kernel(x_ref, g_ref, mod_ref, wr_ref, br_ref,
                   h_ref, mi_ref, mw_ref, cnt_ref, carry_ref):
    i = pl.program_id(0)

    @pl.when(i == 0)
    def _():
        carry_ref[...] = jnp.zeros(carry_ref.shape, F32)

    h = _modulated_norm(x_ref[...], g_ref[...], mod_ref[3:4, :], mod_ref[4:5, :])
    h_ref[...] = h
    lg = _dot(h.astype(BF16), wr_ref[...]) + br_ref[...]
    lane = lax.broadcasted_iota(jnp.int32, lg.shape, 1)
    ninf = -jnp.inf

    gl = jnp.where(lane < N_GROUPS, lg, ninf)
    gm = jnp.max(gl, axis=1, keepdims=True)
    g_sel = jnp.min(jnp.where(gl == gm, lane, LANES), axis=1, keepdims=True)
    p_g = 1.0 / jnp.sum(jnp.exp(gl - gm), axis=1, keepdims=True)

    grp = jnp.right_shift(lane - N_GROUPS, 3)
    in_grp = jnp.logical_and(jnp.logical_and(lane >= N_GROUPS, lane < N_GROUPS + N_EXP), grp == g_sel)
    el = jnp.where(in_grp, lg, ninf)
    m1 = jnp.max(el, axis=1, keepdims=True)
    i1 = jnp.min(jnp.where(el == m1, lane, LANES), axis=1, keepdims=True)
    el2 = jnp.where(lane == i1, ninf, el)
    m2 = jnp.max(el2, axis=1, keepdims=True)
    i2 = jnp.min(jnp.where(el2 == m2, lane, LANES), axis=1, keepdims=True)
    e = jnp.exp(m2 - m1)
    w1 = p_g * (1.0 / (1.0 + e))
    w2 = p_g * (e / (1.0 + e))
    e1 = i1 - N_GROUPS
    e2 = i2 - N_GROUPS

    oh1 = lane == e1
    oh2 = lane == e2
    used = jnp.where(jnp.logical_or(oh1, oh2), 1.0, 0.0)
    ri = lax.broadcasted_iota(jnp.int32, (RT_TM, RT_TM), 0)
    ci = lax.broadcasted_iota(jnp.int32, (RT_TM, RT_TM), 1)
    lower = jnp.where(ci < ri, 1.0, 0.0).astype(BF16)
    before = _dot(lower, used.astype(BF16)) + carry_ref[...]
    r1 = jnp.sum(jnp.where(oh1, before, 0.0), axis=1, keepdims=True).astype(jnp.int32)
    r2 = jnp.sum(jnp.where(oh2, before, 0.0), axis=1, keepdims=True).astype(jnp.int32)
    total = carry_ref[...] + jnp.sum(used, axis=0, keepdims=True)
    carry_ref[...] = total
    cnt_ref[...] = total

    zero = jnp.zeros_like(lane)
    mi_ref[...] = jnp.where(lane == 0, e1, jnp.where(lane == 1, e2,
                            jnp.where(lane == 2, r1, jnp.where(lane == 3, r2, zero))))
    mw_ref[...] = jnp.where(lane == 0, w1, jnp.where(lane == 1, w2, 0.0))


def _router(x1, g, mod3, wr, br):
    tpb = S // RT_TM
    return pl.pallas_call(
        _router_kernel,
        grid=(N // RT_TM,),
        in_specs=[pl.BlockSpec((RT_TM, D), lambda i: (i, 0)),
                  pl.BlockSpec((1, D), lambda i: (0, 0)),
                  pl.BlockSpec((None, N_MOD, D), lambda i: (i // tpb, 0, 0)),
                  pl.BlockSpec((D, LANES), lambda i: (0, 0)),
                  pl.BlockSpec((1, LANES), lambda i: (0, 0))],
        out_specs=[pl.BlockSpec((RT_TM, D), lambda i: (i, 0)),
                   pl.BlockSpec((RT_TM, LANES), lambda i: (i, 0)),
                   pl.BlockSpec((RT_TM, LANES), lambda i: (i, 0)),
                   pl.BlockSpec((1, LANES), lambda i: (0, 0))],
        out_shape=[jax.ShapeDtypeStruct((N, D), F32),
                   jax.ShapeDtypeStruct((N, LANES), jnp.int32),
                   jax.ShapeDtypeStruct((N, LANES), F32),
                   jax.ShapeDtypeStruct((1, LANES), F32)],
        scratch_shapes=[pltpu.VMEM((1, LANES), F32)],
        compiler_params=_params("arbitrary"),
        name="norm2_router",
    )(x1, g.reshape(1, D), mod3, wr, br)


DP_TM = 256


def _row_copy(src_ref, src_row, dst_ref, dst_row, sem):
    return pltpu.make_async_copy(src_ref.at[pl.ds(src_row, 1), :], dst_ref.at[pl.ds(dst_row, 1), :], sem)


def _dispatch_kernel(slot_ref, h_ref, xs_in_ref, xs_ref, sem):
    del xs_in_ref
    base = pl.program_id(0) * DP_TM

    def issue(r, carry):
        for k in range(2):
            _row_copy(h_ref, r, xs_ref, slot_ref[(base + r) * 2 + k], sem).start()
        return carry

    lax.fori_loop(0, DP_TM, issue, 0, unroll=8)

    def drain(r, carry):
        for _ in range(2):
            _row_copy(h_ref, 0, xs_ref, 0, sem).wait()
        return carry

    lax.fori_loop(0, DP_TM, drain, 0, unroll=8)


def _dispatch(slot, h2, xs0):
    grid_spec = pltpu.PrefetchScalarGridSpec(
        num_scalar_prefetch=1,
        grid=(N // DP_TM,),
        in_specs=[pl.BlockSpec((DP_TM, D), lambda i, *_: (i, 0)),
                  pl.BlockSpec(memory_space=pl.ANY)],
        out_specs=pl.BlockSpec(memory_space=pl.ANY),
        scratch_shapes=[pltpu.SemaphoreType.DMA(())],
    )
    return pl.pallas_call(
        _dispatch_kernel,
        grid_spec=grid_spec,
        out_shape=jax.ShapeDtypeStruct((XS_ROWS, D), F32),
        input_output_aliases={2: 0},
        compiler_params=_params("arbitrary"),
        name="moe_dispatch",
    )(slot, h2, xs0)


def _expert_kernel(ce_ref, nc_ref, x_ref, wg_ref, wu_ref, wd_ref, y_ref, wg_s, wu_s, wd_s):
    c = pl.program_id(0)
    valid = c < nc_ref[0]
    new_expert = jnp.logical_or(c == 0, ce_ref[c] != ce_ref[jnp.maximum(c - 1, 0)])

    @pl.when(jnp.logical_and(valid, new_expert))
    def _():
        wg_s[...] = wg_ref[...].astype(BF16)
        wu_s[...] = wu_ref[...].astype(BF16)
        wd_s[...] = wd_ref[...].astype(BF16)

    @pl.when(valid)
    def _():
        x = x_ref[...].astype(BF16)
        a = _dot(x, wg_s[...])
        u = _dot(x, wu_s[...])
        act = (a * jax.nn.sigmoid(a)) * u
        y_ref[...] = _dot(act.astype(BF16), wd_s[...])


def _experts(ce, nc, xs, wg, wu, wd):
    rows = lambda c, ce, nc: (jnp.minimum(c, nc[0] - 1), 0)
    wmap = lambda c, ce, nc: (ce[c], 0, 0)
    grid_spec = pltpu.PrefetchScalarGridSpec(
        num_scalar_prefetch=2,
        grid=(MAX_CHUNKS,),
        in_specs=[pl.BlockSpec((CH, D), rows),
                  pl.BlockSpec((None, D, FF), wmap),
                  pl.BlockSpec((None, D, FF), wmap),
                  pl.BlockSpec((None, FF, D), wmap)],
        out_specs=pl.BlockSpec((CH, D), rows),
        scratch_shapes=[pltpu.VMEM((D, FF), BF16), pltpu.VMEM((D, FF), BF16), pltpu.VMEM((FF, D), BF16)],
    )
    return pl.pallas_call(
        _expert_kernel,
        grid_spec=grid_spec,
        out_shape=jax.ShapeDtypeStruct((XS_ROWS, D), F32),
        input_output_aliases={2: 0},
        compiler_params=_params("arbitrary"),
        name="moe_experts",
    )(ce, nc, xs, wg, wu, wd)


CB_TM = 256


def _combine_kernel(slot_ref, y_ref, x_ref, mw_ref, mod_ref, g_ref, o_ref, ybuf, sem):
    i = pl.program_id(0)
    cur = i % 2

    def issue(tile, buf):
        def body(r, carry):
            for k in range(2):
                _row_copy(y_ref, slot_ref[(tile * CB_TM + r) * 2 + k], ybuf.at[buf, k], r, sem.at[buf]).start()
            return carry

        lax.fori_loop(0, CB_TM, body, 0, unroll=8)

    @pl.when(i == 0)
    def _():
        issue(0, 0)

    @pl.when(i + 1 < pl.num_programs(0))
    def _():
        issue(i + 1, 1 - cur)

    def drain(r, carry):
        for k in range(2):
            _row_copy(y_ref, 0, ybuf.at[cur, k], 0, sem.at[cur]).wait()
        return carry

    lax.fori_loop(0, CB_TM, drain, 0, unroll=8)

    moe = mw_ref[:, 0:1] * ybuf[cur, 0] + mw_ref[:, 1:2] * ybuf[cur, 1]
    x = x_ref[...] + mod_ref[5:6, :] * moe
    r = lax.rsqrt(jnp.mean(x * x, axis=-1, keepdims=True) + EPS)
    o_ref[...] = (x * r) * g_ref[...]


def _combine(slot, y, x1, mw, mod3, g):
    tpb = S // CB_TM
    grid_spec = pltpu.PrefetchScalarGridSpec(
        num_scalar_prefetch=1,
        grid=(N // CB_TM,),
        in_specs=[pl.BlockSpec(memory_space=pl.ANY),
                  pl.BlockSpec((CB_TM, D), lambda i, *_: (i, 0)),
                  pl.BlockSpec((CB_TM, LANES), lambda i, *_: (i, 0)),
                  pl.BlockSpec((None, N_MOD, D), lambda i, *_: (i // tpb, 0, 0)),
                  pl.BlockSpec((1, D), lambda i, *_: (0, 0))],
        out_specs=pl.BlockSpec((CB_TM, D), lambda i, *_: (i, 0)),
        scratch_shapes=[pltpu.VMEM((2, 2, CB_TM, D), F32), pltpu.SemaphoreType.DMA((2,))],
    )
    return pl.pallas_call(
        _combine_kernel,
        grid_spec=grid_spec,
        out_shape=jax.ShapeDtypeStruct((N, D), F32),
        compiler_params=_params("arbitrary"),
        name="moe_combine_norm",
    )(slot, y, x1, mw, mod3, g.reshape(1, D))


def kernel(x, c, w_ada, b_ada, g_norm_mix, w_in, conv_w, w_a_up, w_b_out, w_o, g_norm_ffn,
           w_rg, b_rg, w_re, b_re, w_e_gate, w_e_up, w_e_down, g_norm_final):
    x2 = x.reshape(N, D)
    mod3 = _ada(c, w_ada[0], b_ada[0]).reshape(B, N_MOD, D)

    wi = w_in[0]
    o_q, o_k, o_v, o_iq, o_ik, o_iw, o_bg = 0, 1024, 1280, 1536, 2560, 2624, 2640
    w_a = jnp.concatenate([wi[:, o_q:o_k], wi[:, o_iq:o_ik], wi[:, o_k:o_iq], wi[:, o_ik:o_iw],
                           jnp.zeros((D, LANES - IDX_DIM), F32)], axis=1).astype(BF16)
    w_iw_t = jnp.pad(wi[:, o_iw:o_bg].T, ((0, LANES - IDX_HEADS), (0, 0))).astype(BF16)
    w_b = wi[:, o_bg:].astype(BF16)

    h, pa, iwt = _norm_proj(x2, g_norm_mix[0], mod3, w_a, w_iw_t)
    pb = _matmul(h, w_b, F32, 1024, 512, "proj_conv_gates")

    oa = _attention(pa, iwt)
    zb = _short_conv(pb, conv_w[0])
    merged = _merge(oa, zb, w_a_up[0].astype(BF16), w_b_out[0].astype(BF16), pb)
    x1 = _out_proj(merged, w_o[0].astype(BF16), x2, mod3)

    wr = jnp.pad(jnp.concatenate([w_rg[0], w_re[0]], axis=1),
                 ((0, 0), (0, LANES - N_GROUPS - N_EXP))).astype(BF16)
    br = jnp.pad(jnp.concatenate([b_rg[0], b_re[0]]), (0, LANES - N_GROUPS - N_EXP)).reshape(1, LANES)
    h2, mi, mw, cnt = _router(x1, g_norm_ffn[0], mod3, wr, br)

    counts = cnt[0, :N_EXP].astype(jnp.int32)
    nch = (counts + CH - 1) // CH
    cum = jnp.cumsum(nch)
    offs = ((cum - nch) * CH).astype(jnp.int32)
    total = cum[-1]
    cidx = jnp.minimum(jnp.arange(MAX_CHUNKS, dtype=jnp.int32), total - 1)
    ce = jnp.minimum(jnp.sum((cum[None, :] <= cidx[:, None]).astype(jnp.int32), axis=1), N_EXP - 1)
    nc = total.reshape(1).astype(jnp.int32)
    seg = jnp.sum(jnp.where(mi[:, 0:2, None] == jnp.arange(N_EXP, dtype=jnp.int32), offs, 0), axis=-1)
    slot = (seg + mi[:, 2:4]).reshape(-1)

    xs = _dispatch(slot, h2, jnp.zeros((XS_ROWS, D), F32))
    y = _experts(ce, nc, xs, w_e_gate[0], w_e_up[0], w_e_down[0])
    out = _combine(slot, y, x1, mw, mod3, g_norm_final)
    return out.reshape(B, S, D)
```

```python
import functools

import jax
import jax.numpy as jnp
from jax import lax
from jax.experimental import pallas as pl
from jax.experimental.pallas import tpu as pltpu

F32 = jnp.float32
BF16 = jnp.bfloat16

D = 2048
B = 2
S = 4096
N = B * S
EPS = 1e-6
HEAD_DIM = 64
ATTN_W = 1024
N_KV = 4
KV_W = 256
IDX_HEADS = 16
IDX_DIM = 64
TOPK = min(256, S // 4)
CONV_W = 1024
N_GROUPS = 4
EPG = 8
N_EXP = 32
FF = 512
N_MOD = 6

LANES = 128
TQ = 128
TK = 512
CH = 576
MAX_CHUNKS = (N * 2) // CH + N_EXP
XS_ROWS = MAX_CHUNKS * CH
MASKED = -1e30
BISECT_MAX_IT = 320
TIE_STEPS = 13
CNT_ROWS = 64
VMEM_LIMIT = 56 * 1024 * 1024


def _dot(a, b):
    return jnp.dot(a, b, preferred_element_type=F32)


def _dot_nt(a, b):
    return lax.dot_general(a, b, (((1,), (1,)), ((), ())), preferred_element_type=F32)


def _params(*sem):
    return pltpu.CompilerParams(dimension_semantics=sem, vmem_limit_bytes=VMEM_LIMIT)


ADA_TN = 1024
ADA_KC = 256


def _ada_kernel(ct_ref, w_ref, b_ref, o_ref):
    tn = w_ref.shape[1]

    def body(i, accs):
        k0 = pl.multiple_of(i * ADA_KC, ADA_KC)
        ct = ct_ref[pl.ds(k0, ADA_KC), :]
        act = ct * jax.nn.sigmoid(ct)
        w = w_ref[pl.ds(k0, ADA_KC), :]
        out = []
        for b in range(B):
            prod = w * act[:, b:b + 1]
            out.append(accs[b] + prod.reshape(ADA_KC // 8, 8, tn).sum(axis=0))
        return tuple(out)

    accs = lax.fori_loop(0, D // ADA_KC, body, tuple(jnp.zeros((8, tn), F32) for _ in range(B)))
    rows = [jnp.sum(a, axis=0, keepdims=True) for a in accs]
    o_ref[...] = jnp.concatenate(rows, axis=0) + b_ref[...]


def _ada(c, w, b):
    n_out = w.shape[1]
    return pl.pallas_call(
        _ada_kernel,
        grid=(n_out // ADA_TN,),
        in_specs=[pl.BlockSpec((D, B), lambda j: (0, 0)),
                  pl.BlockSpec((D, ADA_TN), lambda j: (0, j)),
                  pl.BlockSpec((1, ADA_TN), lambda j: (0, j))],
        out_specs=pl.BlockSpec((B, ADA_TN), lambda j: (0, j)),
        out_shape=jax.ShapeDtypeStruct((B, n_out), F32),
        compiler_params=_params("arbitrary"),
        name="ada_mod",
    )(c.T, w, b.reshape(1, n_out))


NORM_TM = 512


def _modulated_norm(x, g, shift, scale):
    r = lax.rsqrt(jnp.mean(x * x, axis=-1, keepdims=True) + EPS)
    return (x * r) * g * (1.0 + scale) + shift


def _norm_proj_kernel(x_ref, g_ref, mod_ref, wa_ref, h_ref, pa_ref, tail_ref):
    h = _modulated_norm(x_ref[...], g_ref[...], mod_ref[0:1, :], mod_ref[1:2, :]).astype(BF16)
    h_ref[...] = h
    p = _dot(h, wa_ref[...])
    pa_ref[...] = p.astype(pa_ref.dtype)
    tail_ref[...] = p[:, A_COLS - LANES:]


def _norm_proj(x2, g, mod3, w_bf):
    tpb = S // NORM_TM
    n_a = A_COLS
    return pl.pallas_call(
        _norm_proj_kernel,
        grid=(N // NORM_TM,),
        in_specs=[pl.BlockSpec((NORM_TM, D), lambda i: (i, 0)),
                  pl.BlockSpec((1, D), lambda i: (0, 0)),
                  pl.BlockSpec((None, N_MOD, D), lambda i: (i // tpb, 0, 0)),
                  pl.BlockSpec((None, D, n_a), lambda i: (0, 0, 0))],
        out_specs=[pl.BlockSpec((NORM_TM, D), lambda i: (i, 0)),
                   pl.BlockSpec((NORM_TM, n_a), lambda i: (i, 0)),
                   pl.BlockSpec((NORM_TM, LANES), lambda i: (i, 0))],
        out_shape=[jax.ShapeDtypeStruct((N, D), BF16),
                   jax.ShapeDtypeStruct((N, n_a), BF16),
                   jax.ShapeDtypeStruct((N, LANES), F32)],
        compiler_params=_params("arbitrary"),
        name="norm1_proj_attn",
    )(x2, g.reshape(1, D), mod3, w_bf)


def _mm_kernel(a_ref, w_ref, o_ref):
    o_ref[...] = _dot(a_ref[...], w_ref[...]).astype(o_ref.dtype)


def _matmul(a, w, out_dtype, tm, tn, name):
    m, k = a.shape
    n = w.shape[1]
    return pl.pallas_call(
        _mm_kernel,
        grid=(m // tm, n // tn),
        in_specs=[pl.BlockSpec((tm, k), lambda i, j: (i, 0)),
                  pl.BlockSpec((k, tn), lambda i, j: (0, j))],
        out_specs=pl.BlockSpec((tm, tn), lambda i, j: (i, j)),
        out_shape=jax.ShapeDtypeStruct((m, n), out_dtype),
        compiler_params=_params("arbitrary", "arbitrary"),
        name=name,
    )(a, w)


A_COLS = 2 * ATTN_W + 2 * KV_W + LANES


def _spread_matrix(groups):
    r = lax.broadcasted_iota(jnp.int32, (groups * HEAD_DIM, groups * 4 * HEAD_DIM), 0)
    c = lax.broadcasted_iota(jnp.int32, (groups * HEAD_DIM, groups * 4 * HEAD_DIM), 1)
    d, cc = r & (HEAD_DIM - 1), c & (4 * HEAD_DIM - 1)
    hit = jnp.logical_and(jnp.right_shift(r, 6) == jnp.right_shift(c, 8),
                          jnp.logical_or(cc == d, cc == d + 3 * HEAD_DIM))
    return jnp.where(hit, 1.0, 0.0).astype(BF16)


def _attn_kernel(q_ref, iq_lo_ref, iq_hi_ref, tail_ref, ik_ref, k_ref, v_ref, o_ref,
                 s_ref, thr_ref, tie_ref, qs_ref, iqs_ref, ikp_ref, kp_ref, vaug_ref, acc_ref,
                 lg0_ref, a0_ref, m0_ref, lg1_ref, a1_ref, m1_ref):
    qi = pl.program_id(1)
    nkc = qi // (TK // TQ) + 1

    @pl.when(qi == 0)
    def _():
        spread_k = _spread_matrix(N_KV)
        spread_ik = _spread_matrix(2)[0:LANES, 0:2 * LANES]
        lane_v = lax.broadcasted_iota(jnp.int32, (TK, LANES), 1)
        ones = jnp.ones((TK, LANES), BF16)

        def fill(c, carry):
            off = pl.multiple_of(c * TK, TK)
            kp_ref[pl.ds(off, TK), :] = _dot(k_ref[pl.ds(off, TK), :], spread_k).astype(BF16)
            ikp_ref[pl.ds(off, TK), :] = _dot(ik_ref[pl.ds(off, TK), :], spread_ik).astype(BF16)
            for g in range(N_KV):
                src = v_ref[pl.ds(off, TK), LANES * (g // 2):LANES * (g // 2 + 1)]
                keep = (lane_v < 64) if g % 2 == 0 else (lane_v >= 64)
                vaug_ref[pl.ds(off, TK), LANES * g:LANES * (g + 1)] = jnp.where(keep, src, ones)
            return carry

        lax.fori_loop(0, S // TK, fill, 0)

    ident = jnp.where(lax.broadcasted_iota(jnp.int32, (TQ, LANES), 0)
                      == lax.broadcasted_iota(jnp.int32, (TQ, LANES), 1), 1.0, 0.0).astype(BF16)
    for j in range(8):
        iq_half = iq_lo_ref if j < 4 else iq_hi_ref
        iqs_ref[j * TQ:(j + 1) * TQ, :] = iq_half[:, LANES * (j % 4):LANES * (j % 4 + 1)]
        qs_ref[j * TQ:(j + 1) * TQ, 0:LANES] = q_ref[:, LANES * j:LANES * (j + 1)] * (HEAD_DIM ** -0.5)
        qs_ref[j * TQ:(j + 1) * TQ, LANES:2 * LANES] = ident

    w = tail_ref[...].T[IDX_DIM:IDX_DIM + IDX_HEADS, :] * ((IDX_HEADS ** -0.5) * (IDX_DIM ** -0.5))

    keyi = lax.broadcasted_iota(jnp.int32, (TK, TQ), 0)
    qryi = qi * TQ + lax.broadcasted_iota(jnp.int32, (TK, TQ), 1)

    def score_body(c, carry):
        lo, hi = carry
        off = pl.multiple_of(c * TK, TK)
        ik_even = ikp_ref[pl.ds(off, TK), 0:LANES]
        ik_odd = ikp_ref[pl.ds(off, TK), LANES:2 * LANES]
        acc = jnp.zeros((TK, TQ), F32)
        for jj in range(4):
            iq2 = iqs_ref[jj * 2 * TQ:(jj + 1) * 2 * TQ, :]
            for par, ik_par in ((0, ik_even), (1, ik_odd)):
                d = _dot_nt(ik_par, iq2)
                for t in range(2):
                    h = 2 * (2 * jj + t) + par
                    acc = acc + jnp.maximum(d[:, t * TQ:(t + 1) * TQ], 0.0) * w[h:h + 1, :]
        valid = (off + keyi) <= qryi
        s_ref[pl.ds(off, TK), :] = jnp.where(valid, acc, -jnp.inf)
        hi = jnp.maximum(hi, jnp.max(jnp.where(valid, acc, -jnp.inf), axis=0, keepdims=True))
        lo = jnp.minimum(lo, jnp.min(jnp.where(valid, acc, jnp.inf), axis=0, keepdims=True))
        return lo, hi

    lo0, hi0 = lax.fori_loop(0, nkc, score_body,
                             (jnp.full((1, TQ), jnp.inf, F32), jnp.full((1, TQ), -jnp.inf, F32)))

    def count_keys(pred):
        def body(c, cnt):
            off = pl.multiple_of(c * TK, TK)
            hit = jnp.where(pred(s_ref[pl.ds(off, TK), :], off + keyi), 1.0, 0.0)
            return cnt + hit.reshape(TK // CNT_ROWS, CNT_ROWS, TQ).sum(axis=0)

        cnt = lax.fori_loop(0, nkc, body, jnp.zeros((CNT_ROWS, TQ), F32))
        return jnp.sum(cnt, axis=0, keepdims=True)

    def count_ge(mid):
        return count_keys(lambda sc, _: sc >= mid)

    tie_ref[...] = jnp.full((8, TQ), S, jnp.int32)

    @pl.when(qi * TQ + TQ <= TOPK)
    def _():
        thr_ref[...] = jnp.full((8, TQ), jnp.finfo(F32).min, F32)

    @pl.when(qi * TQ + TQ > TOPK)
    def _():
        kf = float(TOPK)

        def step(lo, hi):
            mid = lo + (hi - lo) * 0.5
            cnt = count_ge(mid)
            ge = cnt >= kf
            eq = cnt == kf
            lo2 = jnp.where(ge, mid, lo)
            hi2 = jnp.where(eq, mid, jnp.where(ge, hi, mid))
            stuck = jnp.logical_or(mid <= lo, mid >= hi)
            return lo2, hi2, jnp.logical_and(lo2 < hi2, jnp.logical_not(stuck))

        def cond(st):
            it, _, _, active = st
            return jnp.logical_and(it < BISECT_MAX_IT, active > 0.0)

        def body(st):
            it, lo, hi, _ = st
            lo, hi, _ = step(lo, hi)
            lo, hi, act = step(lo, hi)
            return it + 2, lo, hi, jnp.max(jnp.where(act, 1.0, 0.0))

        _, lo, _, _ = lax.while_loop(cond, body, (jnp.int32(0), lo0, hi0, jnp.float32(1.0)))
        thr_ref[...] = jnp.broadcast_to(lo, (8, TQ))

        need = count_ge(lo) > kf

        @pl.when(jnp.max(jnp.where(need, 1.0, 0.0)) > 0.0)
        def _():
            keep = kf - count_keys(lambda sc, _: sc > lo)

            def index_step(_, st):
                below, upto = st
                mid = jnp.right_shift(below + upto, 1)
                ok = count_keys(lambda sc, ki: jnp.logical_and(sc == lo, ki <= mid)) >= keep
                return jnp.where(ok, below, mid), jnp.where(ok, mid, upto)

            _, upto = lax.fori_loop(0, TIE_STEPS, index_step,
                                    (jnp.full((1, TQ), -1, jnp.int32), jnp.full((1, TQ), S - 1, jnp.int32)))
            tie_ref[...] = jnp.broadcast_to(jnp.where(need, upto, S), (8, TQ))

    acc_ref[...] = jnp.zeros(acc_ref.shape, F32)
    thr = thr_ref[0:1, :]
    tie = tie_ref[0:1, :]
    nt = TK // LANES
    n_blk = 2 * N_KV
    bufs = ((lg0_ref, a0_ref, m0_ref), (lg1_ref, a1_ref, m1_ref))
    m1_ref[...] = jnp.full(m1_ref.shape, MASKED, F32)

    def pass_a(c, slot):
        lg_ref, a_ref, m_ref = bufs[slot]
        m_prev_ref = bufs[1 - slot][2]
        off = pl.multiple_of(c * TK, TK)
        sc = s_ref[pl.ds(off, TK), :]
        sel = jnp.logical_or(sc > thr, jnp.logical_and(sc == thr, (off + keyi) <= tie))
        bias_t = jnp.where(sel, 0.0, MASKED).astype(BF16)
        for blk in range(n_blk):
            g, par = blk // 2, blk % 2
            qg = qs_ref[g * 2 * TQ:(g + 1) * 2 * TQ, :]
            kk = kp_ref[pl.ds(off, TK), g * 256 + par * LANES:g * 256 + (par + 1) * LANES]
            lg = _dot_nt(qg, jnp.concatenate([kk, bias_t], axis=1))
            lg_ref[blk] = lg
            m_old = m_prev_ref[blk]
            m_new = jnp.maximum(m_old, jnp.max(lg, axis=1, keepdims=True))
            a_ref[blk] = jnp.exp(m_old - m_new)
            m_ref[blk] = m_new

    def pass_b(c, slot):
        lg_ref, a_ref, m_ref = bufs[slot]
        off = pl.multiple_of(c * TK, TK)
        for blk in range(n_blk):
            g = blk // 2
            m_new = m_ref[blk]
            p = jnp.concatenate([jnp.exp(lg_ref[blk, :, LANES * t:LANES * (t + 1)] - m_new)
                                 for t in range(nt)], axis=1).astype(BF16)
            va = vaug_ref[pl.ds(off, TK), LANES * g:LANES * (g + 1)]
            acc_ref[blk] = a_ref[blk] * acc_ref[blk] + _dot(p, va)

    pass_a(0, 0)

    def pair_body(i, carry):
        c = 2 * i
        pass_a(c + 1, 1)
        pass_b(c, 0)
        pass_a(c + 2, 0)
        pass_b(c + 1, 1)
        return carry

    lax.fori_loop(0, (nkc - 1) // 2, pair_body, 0)

    @pl.when(nkc % 2 == 0)
    def _():
        pass_a(nkc - 1, 1)
        pass_b(nkc - 2, 0)
        pass_b(nkc - 1, 1)

    @pl.when(nkc % 2 == 1)
    def _():
        pass_b(nkc - 1, 0)

    lane = lax.broadcasted_iota(jnp.int32, (TQ, LANES), 1)
    for g in range(N_KV):
        acc_e = acc_ref[2 * g]
        acc_o = acc_ref[2 * g + 1]
        o_even = acc_e / pltpu.roll(acc_e, 64, axis=1)
        o_odd = acc_o / pltpu.roll(acc_o, 64, axis=1)
        for pr in range(2):
            xa = o_even[pr * TQ:(pr + 1) * TQ]
            xb = o_odd[pr * TQ:(pr + 1) * TQ]
            if g % 2 == 0:
                left, right = xa, pltpu.roll(xb, 64, axis=1)
            else:
                left, right = pltpu.roll(xa, 64, axis=1), xb
            col = (2 * g + pr) * LANES
            o_ref[:, col:col + LANES] = jnp.where(lane < 64, left, right).astype(o_ref.dtype)


def _attention(a, tail):
    nq = S // TQ
    rowmap = lambda b, i: b * nq + i
    return pl.pallas_call(
        _attn_kernel,
        grid=(B, nq),
        in_specs=[pl.BlockSpec((TQ, ATTN_W), lambda b, i: (rowmap(b, i), 0)),
                  pl.BlockSpec((TQ, ATTN_W // 2), lambda b, i: (rowmap(b, i), 3)),
                  pl.BlockSpec((TQ, ATTN_W // 2), lambda b, i: (rowmap(b, i), 4)),
                  pl.BlockSpec((TQ, LANES), lambda b, i: (rowmap(b, i), 0)),
                  pl.BlockSpec((S, LANES), lambda b, i: (b, 20)),
                  pl.BlockSpec((S, KV_W), lambda b, i: (b, 4)),
                  pl.BlockSpec((S, KV_W), lambda b, i: (b, 5))],
        out_specs=pl.BlockSpec((TQ, ATTN_W), lambda b, i: (rowmap(b, i), 0)),
        out_shape=jax.ShapeDtypeStruct((N, ATTN_W), BF16),
        scratch_shapes=[pltpu.VMEM((S, TQ), F32),
                        pltpu.VMEM((8, TQ), F32),
                        pltpu.VMEM((8, TQ), jnp.int32),
                        pltpu.VMEM((8 * TQ, 2 * LANES), BF16),
                        pltpu.VMEM((8 * TQ, LANES), BF16),
                        pltpu.VMEM((S, 2 * LANES), BF16),
                        pltpu.VMEM((S, N_KV * 2 * LANES), BF16),
                        pltpu.VMEM((S, N_KV * LANES), BF16),
                        pltpu.VMEM((2 * N_KV, 2 * TQ, LANES), F32)]
                       + 2 * [pltpu.VMEM((2 * N_KV, 2 * TQ, TK), F32),
                              pltpu.VMEM((2 * N_KV, 2 * TQ, LANES), F32),
                              pltpu.VMEM((2 * N_KV, 2 * TQ, LANES), F32)],

        compiler_params=_params("arbitrary", "arbitrary"),
        name="sparse_attn",
    )(a, a, a, tail, a, a, a)


CONV_TS = 512


def _conv_kernel(bg_ref, cg_ref, xc_ref, cw_ref, o_ref, carry_ref):
    i = pl.program_id(0)

    @pl.when(i % (S // CONV_TS) == 0)
    def _():
        carry_ref[...] = jnp.zeros(carry_ref.shape, F32)

    u = cg_ref[...] * xc_ref[...]
    prev = carry_ref[...]
    row = lax.broadcasted_iota(jnp.int32, u.shape, 0)
    u1 = jnp.where(row == 0, prev[7:8, :], pltpu.roll(u, 1, axis=0))
    u2 = jnp.where(row == 0, prev[6:7, :], jnp.where(row == 1, prev[7:8, :], pltpu.roll(u, 2, axis=0)))
    y = cw_ref[0:1, :] * u2 + cw_ref[1:2, :] * u1 + cw_ref[2:3, :] * u
    o_ref[...] = (bg_ref[...] * y).astype(o_ref.dtype)
    carry_ref[...] = u[CONV_TS - 8:CONV_TS, :]


def _short_conv(pb, conv_w):
    return pl.pallas_call(
        _conv_kernel,
        grid=(N // CONV_TS,),
        in_specs=[pl.BlockSpec((CONV_TS, CONV_W), lambda i: (i, 0)),
                  pl.BlockSpec((CONV_TS, CONV_W), lambda i: (i, 1)),
                  pl.BlockSpec((CONV_TS, CONV_W), lambda i: (i, 2)),
                  pl.BlockSpec((3, CONV_W), lambda i: (0, 0))],
        out_specs=pl.BlockSpec((CONV_TS, CONV_W), lambda i: (i, 0)),
        out_shape=jax.ShapeDtypeStruct((N, CONV_W), BF16),
        scratch_shapes=[pltpu.VMEM((8, CONV_W), F32)],
        compiler_params=_params("arbitrary"),
        name="short_conv",
    )(pb, pb, pb, conv_w)


MG_TM = 1024
MG_TN = 512


def _merge_kernel(oa_ref, zb_ref, wa_ref, wb_ref, ga_ref, gb_ref, o_ref):
    ya = _dot(oa_ref[...], wa_ref[...])
    yb = _dot(zb_ref[...], wb_ref[...])
    merged = jax.nn.sigmoid(ga_ref[...]) * ya + jax.nn.sigmoid(gb_ref[...]) * yb
    o_ref[...] = merged.astype(o_ref.dtype)


def _merge(oa, zb, wa, wb, pb):
    ga0 = (3 * CONV_W) // MG_TN
    gb0 = (3 * CONV_W + D) // MG_TN
    return pl.pallas_call(
        _merge_kernel,
        grid=(N // MG_TM, D // MG_TN),
        in_specs=[pl.BlockSpec((MG_TM, ATTN_W), lambda i, j: (i, 0)),
                  pl.BlockSpec((MG_TM, CONV_W), lambda i, j: (i, 0)),
                  pl.BlockSpec((ATTN_W, MG_TN), lambda i, j: (0, j)),
                  pl.BlockSpec((CONV_W, MG_TN), lambda i, j: (0, j)),
                  pl.BlockSpec((MG_TM, MG_TN), lambda i, j: (i, ga0 + j)),
                  pl.BlockSpec((MG_TM, MG_TN), lambda i, j: (i, gb0 + j))],
        out_specs=pl.BlockSpec((MG_TM, MG_TN), lambda i, j: (i, j)),
        out_shape=jax.ShapeDtypeStruct((N, D), BF16),
        compiler_params=_params("arbitrary", "arbitrary"),
        name="gated_merge",
    )(oa, zb, wa, wb, pb, pb)


def _wo_kernel(m_ref, w_ref, x_ref, mod_ref, o_ref):
    o_ref[...] = x_ref[...] + mod_ref[2:3, :] * _dot(m_ref[...], w_ref[...])


WO_TN = 1024


def _out_proj(merged, wo, x2, mod3):
    tpb = S // MG_TM
    return pl.pallas_call(
        _wo_kernel,
        grid=(N // MG_TM, D // WO_TN),
        in_specs=[pl.BlockSpec((MG_TM, D), lambda i, j: (i, 0)),
                  pl.BlockSpec((D, WO_TN), lambda i, j: (0, j)),
                  pl.BlockSpec((MG_TM, WO_TN), lambda i, j: (i, j)),
                  pl.BlockSpec((None, N_MOD, WO_TN), lambda i, j: (i // tpb, 0, j))],
        out_specs=pl.BlockSpec((MG_TM, WO_TN), lambda i, j: (i, j)),
        out_shape=jax.ShapeDtypeStruct((N, D), F32),
        compiler_params=_params("arbitrary", "arbitrary"),
        name="out_proj",
    )(merged, wo, x2, mod3)


RT_TM = 256


def _router_kernel(x_ref, g_ref, mod_ref, wr_ref, br_ref,
                   h_ref, mi_ref, mw_ref, cnt_ref, carry_ref):
    i = pl.program_id(0)

    @pl.when(i == 0)
    def _():
        carry_ref[...] = jnp.zeros(carry_ref.shape, F32)

    h = _modulated_norm(x_ref[...], g_ref[...], mod_ref[3:4, :], mod_ref[4:5, :])
    h_ref[...] = h
    lg = _dot(h.astype(BF16), wr_ref[...]) + br_ref[...]
    lane = lax.broadcasted_iota(jnp.int32, lg.shape, 1)
    ninf = -jnp.inf

    gl = jnp.where(lane < N_GROUPS, lg, ninf)
    gm = jnp.max(gl, axis=1, keepdims=True)
    g_sel = jnp.min(jnp.where(gl == gm, lane, LANES), axis=1, keepdims=True)
    p_g = 1.0 / jnp.sum(jnp.exp(gl - gm), axis=1, keepdims=True)

    grp = jnp.right_shift(lane - N_GROUPS, 3)
    in_grp = jnp.logical_and(jnp.logical_and(lane >= N_GROUPS, lane < N_GROUPS + N_EXP), grp == g_sel)
    el = jnp.where(in_grp, lg, ninf)
    m1 = jnp.max(el, axis=1, keepdims=True)
    i1 = jnp.min(jnp.where(el == m1, lane, LANES), axis=1, keepdims=True)
    el2 = jnp.where(lane == i1, ninf, el)
    m2 = jnp.max(el2, axis=1, keepdims=True)
    i2 = jnp.min(jnp.where(el2 == m2, lane, LANES), axis=1, keepdims=True)
    e = jnp.exp(m2 - m1)
    w1 = p_g * (1.0 / (1.0 + e))
    w2 = p_g * (e / (1.0 + e))
    e1 = i1 - N_GROUPS
    e2 = i2 - N_GROUPS

    oh1 = lane == e1
    oh2 = lane == e2
    used = jnp.where(jnp.logical_or(oh1, oh2), 1.0, 0.0)
    ri = lax.broadcasted_iota(jnp.int32, (RT_TM, RT_TM), 0)
    ci = lax.broadcasted_iota(jnp.int32, (RT_TM, RT_TM), 1)
    lower = jnp.where(ci < ri, 1.0, 0.0).astype(BF16)
    before = _dot(lower, used.astype(BF16)) + carry_ref[...]
    r1 = jnp.sum(jnp.where(oh1, before, 0.0), axis=1, keepdims=True).astype(jnp.int32)
    r2 = jnp.sum(jnp.where(oh2, before, 0.0), axis=1, keepdims=True).astype(jnp.int32)
    total = carry_ref[...] + jnp.sum(used, axis=0, keepdims=True)
    carry_ref[...] = total
    cnt_ref[...] = total

    zero = jnp.zeros_like(lane)
    mi_ref[...] = jnp.where(lane == 0, e1, jnp.where(lane == 1, e2,
                            jnp.where(lane == 2, r1, jnp.where(lane == 3, r2, zero))))
    mw_ref[...] = jnp.where(lane == 0, w1, jnp.where(lane == 1, w2, 0.0))


def _router(x1, g, mod3, wr, br):
    tpb = S // RT_TM
    return pl.pallas_call(
        _router_kernel,
        grid=(N // RT_TM,),
        in_specs=[pl.BlockSpec((RT_TM, D), lambda i: (i, 0)),
                  pl.BlockSpec((1, D), lambda i: (0, 0)),
                  pl.BlockSpec((None, N_MOD, D), lambda i: (i // tpb, 0, 0)),
                  pl.BlockSpec((D, LANES), lambda i: (0, 0)),
                  pl.BlockSpec((1, LANES), lambda i: (0, 0))],
        out_specs=[pl.BlockSpec((RT_TM, D), lambda i: (i, 0)),
                   pl.BlockSpec((RT_TM, LANES), lambda i: (i, 0)),
                   pl.BlockSpec((RT_TM, LANES), lambda i: (i, 0)),
                   pl.BlockSpec((1, LANES), lambda i: (0, 0))],
        out_shape=[jax.ShapeDtypeStruct((N, D), F32),
                   jax.ShapeDtypeStruct((N, LANES), jnp.int32),
                   jax.ShapeDtypeStruct((N, LANES), F32),
                   jax.ShapeDtypeStruct((1, LANES), F32)],
        scratch_shapes=[pltpu.VMEM((1, LANES), F32)],
        compiler_params=_params("arbitrary"),
        name="norm2_router",
    )(x1, g.reshape(1, D), mod3, wr, br)


DP_TM = 256


def _row_copy(src_ref, src_row, dst_ref, dst_row, sem):
    return pltpu.make_async_copy(src_ref.at[pl.ds(src_row, 1), :], dst_ref.at[pl.ds(dst_row, 1), :], sem)


def _dispatch_kernel(slot_ref, h_ref, xs_in_ref, xs_ref, sem):
    del xs_in_ref
    base = pl.program_id(0) * DP_TM

    def issue(r, carry):
        for k in range(2):
            _row_copy(h_ref, r, xs_ref, slot_ref[(base + r) * 2 + k], sem).start()
        return carry

    lax.fori_loop(0, DP_TM, issue, 0, unroll=8)

    def drain(r, carry):
        for _ in range(2):
            _row_copy(h_ref, 0, xs_ref, 0, sem).wait()
        return carry

    lax.fori_loop(0, DP_TM, drain, 0, unroll=8)


def _dispatch(slot, h2, xs0):
    grid_spec = pltpu.PrefetchScalarGridSpec(
        num_scalar_prefetch=1,
        grid=(N // DP_TM,),
        in_specs=[pl.BlockSpec((DP_TM, D), lambda i, *_: (i, 0)),
                  pl.BlockSpec(memory_space=pl.ANY)],
        out_specs=pl.BlockSpec(memory_space=pl.ANY),
        scratch_shapes=[pltpu.SemaphoreType.DMA(())],
    )
    return pl.pallas_call(
        _dispatch_kernel,
        grid_spec=grid_spec,
        out_shape=jax.ShapeDtypeStruct((XS_ROWS, D), F32),
        input_output_aliases={2: 0},
        compiler_params=_params("arbitrary"),
        name="moe_dispatch",
    )(slot, h2, xs0)


def _expert_kernel(ce_ref, nc_ref, x_ref, wg_ref, wu_ref, wd_ref, y_ref, wg_s, wu_s, wd_s):
    c = pl.program_id(0)
    valid = c < nc_ref[0]
    new_expert = jnp.logical_or(c == 0, ce_ref[c] != ce_ref[jnp.maximum(c - 1, 0)])

    @pl.when(jnp.logical_and(valid, new_expert))
    def _():
        wg_s[...] = wg_ref[...].astype(BF16)
        wu_s[...] = wu_ref[...].astype(BF16)
        wd_s[...] = wd_ref[...].astype(BF16)

    @pl.when(valid)
    def _():
        x = x_ref[...].astype(BF16)
        a = _dot(x, wg_s[...])
        u = _dot(x, wu_s[...])
        act = (a * jax.nn.sigmoid(a)) * u
        y_ref[...] = _dot(act.astype(BF16), wd_s[...])


def _experts(ce, nc, xs, wg, wu, wd):
    rows = lambda c, ce, nc: (jnp.minimum(c, nc[0] - 1), 0)
    wmap = lambda c, ce, nc: (ce[c], 0, 0)
    grid_spec = pltpu.PrefetchScalarGridSpec(
        num_scalar_prefetch=2,
        grid=(MAX_CHUNKS,),
        in_specs=[pl.BlockSpec((CH, D), rows),
                  pl.BlockSpec((None, D, FF), wmap),
                  pl.BlockSpec((None, D, FF), wmap),
                  pl.BlockSpec((None, FF, D), wmap)],
        out_specs=pl.BlockSpec((CH, D), rows),
        scratch_shapes=[pltpu.VMEM((D, FF), BF16), pltpu.VMEM((D, FF), BF16), pltpu.VMEM((FF, D), BF16)],
    )
    return pl.pallas_call(
        _expert_kernel,
        grid_spec=grid_spec,
        out_shape=jax.ShapeDtypeStruct((XS_ROWS, D), F32),
        input_output_aliases={2: 0},
        compiler_params=_params("arbitrary"),
        name="moe_experts",
    )(ce, nc, xs, wg, wu, wd)


CB_TM = 256


def _combine_kernel(slot_ref, y_ref, x_ref, mw_ref, mod_ref, g_ref, o_ref, ybuf, sem):
    i = pl.program_id(0)
    cur = i % 2

    def issue(tile, buf):
        def body(r, carry):
            for k in range(2):
                _row_copy(y_ref, slot_ref[(tile * CB_TM + r) * 2 + k], ybuf.at[buf, k], r, sem.at[buf]).start()
            return carry

        lax.fori_loop(0, CB_TM, body, 0, unroll=8)

    @pl.when(i == 0)
    def _():
        issue(0, 0)

    @pl.when(i + 1 < pl.num_programs(0))
    def _():
        issue(i + 1, 1 - cur)

    def drain(r, carry):
        for k in range(2):
            _row_copy(y_ref, 0, ybuf.at[cur, k], 0, sem.at[cur]).wait()
        return carry

    lax.fori_loop(0, CB_TM, drain, 0, unroll=8)

    moe = mw_ref[:, 0:1] * ybuf[cur, 0] + mw_ref[:, 1:2] * ybuf[cur, 1]
    x = x_ref[...] + mod_ref[5:6, :] * moe
    r = lax.rsqrt(jnp.mean(x * x, axis=-1, keepdims=True) + EPS)
    o_ref[...] = (x * r) * g_ref[...]


def _combine(slot, y, x1, mw, mod3, g):
    tpb = S // CB_TM
    grid_spec = pltpu.PrefetchScalarGridSpec(
        num_scalar_prefetch=1,
        grid=(N // CB_TM,),
        in_specs=[pl.BlockSpec(memory_space=pl.ANY),
                  pl.BlockSpec((CB_TM, D), lambda i, *_: (i, 0)),
                  pl.BlockSpec((CB_TM, LANES), lambda i, *_: (i, 0)),
                  pl.BlockSpec((None, N_MOD, D), lambda i, *_: (i // tpb, 0, 0)),
                  pl.BlockSpec((1, D), lambda i, *_: (0, 0))],
        out_specs=pl.BlockSpec((CB_TM, D), lambda i, *_: (i, 0)),
        scratch_shapes=[pltpu.VMEM((2, 2, CB_TM, D), F32), pltpu.SemaphoreType.DMA((2,))],
    )
    return pl.pallas_call(
        _combine_kernel,
        grid_spec=grid_spec,
        out_shape=jax.ShapeDtypeStruct((N, D), F32),
        compiler_params=_params("arbitrary"),
        name="moe_combine_norm",
    )(slot, y, x1, mw, mod3, g.reshape(1, D))


def kernel(x, c, w_ada, b_ada, g_norm_mix, w_in, conv_w, w_a_up, w_b_out, w_o, g_norm_ffn,
           w_rg, b_rg, w_re, b_re, w_e_gate, w_e_up, w_e_down, g_norm_final):
    x2 = x.reshape(N, D)
    mod3 = _ada(c, w_ada[0], b_ada[0]).reshape(B, N_MOD, D)

    o_bg = 2640
    w_bf = w_in.astype(BF16)
    w_b = w_bf[0, :, o_bg:]

    h, pa, tail = _norm_proj(x2, g_norm_mix[0], mod3, w_bf)
    pb = _matmul(h, w_b, F32, 1024, 1024, "proj_conv_gates")

    oa = _attention(pa, tail)
    zb = _short_conv(pb, conv_w[0])
    merged = _merge(oa, zb, w_a_up[0].astype(BF16), w_b_out[0].astype(BF16), pb)
    x1 = _out_proj(merged, w_o[0].astype(BF16), x2, mod3)

    wr = jnp.pad(jnp.concatenate([w_rg[0], w_re[0]], axis=1),
                 ((0, 0), (0, LANES - N_GROUPS - N_EXP))).astype(BF16)
    br = jnp.pad(jnp.concatenate([b_rg[0], b_re[0]]), (0, LANES - N_GROUPS - N_EXP)).reshape(1, LANES)
    h2, mi, mw, cnt = _router(x1, g_norm_ffn[0], mod3, wr, br)

    counts = cnt[0, :N_EXP].astype(jnp.int32)
    nch = (counts + CH - 1) // CH
    cum = jnp.cumsum(nch)
    offs = ((cum - nch) * CH).astype(jnp.int32)
    total = cum[-1]
    cidx = jnp.minimum(jnp.arange(MAX_CHUNKS, dtype=jnp.int32), total - 1)
    ce = jnp.minimum(jnp.sum((cum[None, :] <= cidx[:, None]).astype(jnp.int32), axis=1), N_EXP - 1)
    nc = total.reshape(1).astype(jnp.int32)
    seg = jnp.sum(jnp.where(mi[:, 0:2, None] == jnp.arange(N_EXP, dtype=jnp.int32), offs, 0), axis=-1)
    slot = (seg + mi[:, 2:4]).reshape(-1)

    xs = _dispatch(slot, h2, jnp.zeros((XS_ROWS, D), F32))
    y = _experts(ce, nc, xs, w_e_gate[0], w_e_up[0], w_e_down[0])
    out = _combine(slot, y, x1, mw, mod3, g_norm_final)
    return out.reshape(B, S, D)
```

```python
import functools

import jax
import jax.numpy as jnp
from jax import lax
from jax.experimental import pallas as pl
from jax.experimental.pallas import tpu as pltpu

F32 = jnp.float32
BF16 = jnp.bfloat16

D = 2048
B = 2
S = 4096
N = B * S
EPS = 1e-6
HEAD_DIM = 64
ATTN_W = 1024
N_KV = 4
KV_W = 256
IDX_HEADS = 16
IDX_DIM = 64
TOPK = min(256, S // 4)
CONV_W = 1024
N_GROUPS = 4
EPG = 8
N_EXP = 32
FF = 512
N_MOD = 6

LANES = 128
TQ = 128
TK = 512
CH = 576
MAX_CHUNKS = (N * 2) // CH + N_EXP
XS_ROWS = MAX_CHUNKS * CH
MASKED = -1e30
BISECT_MAX_IT = 320
TIE_STEPS = 13
CNT_ROWS = 64
VMEM_LIMIT = 56 * 1024 * 1024


def _dot(a, b):
    return jnp.dot(a, b, preferred_element_type=F32)


def _dot_nt(a, b):
    return lax.dot_general(a, b, (((1,), (1,)), ((), ())), preferred_element_type=F32)


def _params(*sem):
    return pltpu.CompilerParams(dimension_semantics=sem, vmem_limit_bytes=VMEM_LIMIT)


ADA_TN = 1024
ADA_KC = 256


def _ada_kernel(ct_ref, w_ref, b_ref, o_ref):
    tn = w_ref.shape[1]

    def body(i, accs):
        k0 = pl.multiple_of(i * ADA_KC, ADA_KC)
        ct = ct_ref[pl.ds(k0, ADA_KC), :]
        act = ct * jax.nn.sigmoid(ct)
        w = w_ref[pl.ds(k0, ADA_KC), :]
        out = []
        for b in range(B):
            prod = w * act[:, b:b + 1]
            out.append(accs[b] + prod.reshape(ADA_KC // 8, 8, tn).sum(axis=0))
        return tuple(out)

    accs = lax.fori_loop(0, D // ADA_KC, body, tuple(jnp.zeros((8, tn), F32) for _ in range(B)))
    rows = [jnp.sum(a, axis=0, keepdims=True) for a in accs]
    o_ref[...] = jnp.concatenate(rows, axis=0) + b_ref[...]


def _ada(c, w, b):
    n_out = w.shape[1]
    return pl.pallas_call(
        _ada_kernel,
        grid=(n_out // ADA_TN,),
        in_specs=[pl.BlockSpec((D, B), lambda j: (0, 0)),
                  pl.BlockSpec((D, ADA_TN), lambda j: (0, j)),
                  pl.BlockSpec((1, ADA_TN), lambda j: (0, j))],
        out_specs=pl.BlockSpec((B, ADA_TN), lambda j: (0, j)),
        out_shape=jax.ShapeDtypeStruct((B, n_out), F32),
        compiler_params=_params("arbitrary"),
        name="ada_mod",
    )(c.T, w, b.reshape(1, n_out))


NORM_TM = 512


def _modulated_norm(x, g, shift, scale):
    r = lax.rsqrt(jnp.mean(x * x, axis=-1, keepdims=True) + EPS)
    return (x * r) * g * (1.0 + scale) + shift


def _norm_proj_kernel(x_ref, g_ref, mod_ref, wa_ref, h_ref, pa_ref, tail_ref):
    h = _modulated_norm(x_ref[...], g_ref[...], mod_ref[0:1, :], mod_ref[1:2, :]).astype(BF16)
    h_ref[...] = h
    p = _dot(h, wa_ref[...])
    pa_ref[...] = p.astype(pa_ref.dtype)
    tail_ref[...] = p[:, A_COLS - LANES:]


def _norm_proj(x2, g, mod3, w_bf):
    tpb = S // NORM_TM
    n_a = A_COLS
    return pl.pallas_call(
        _norm_proj_kernel,
        grid=(N // NORM_TM,),
        in_specs=[pl.BlockSpec((NORM_TM, D), lambda i: (i, 0)),
                  pl.BlockSpec((1, D), lambda i: (0, 0)),
                  pl.BlockSpec((None, N_MOD, D), lambda i: (i // tpb, 0, 0)),
                  pl.BlockSpec((None, D, n_a), lambda i: (0, 0, 0))],
        out_specs=[pl.BlockSpec((NORM_TM, D), lambda i: (i, 0)),
                   pl.BlockSpec((NORM_TM, n_a), lambda i: (i, 0)),
                   pl.BlockSpec((NORM_TM, LANES), lambda i: (i, 0))],
        out_shape=[jax.ShapeDtypeStruct((N, D), BF16),
                   jax.ShapeDtypeStruct((N, n_a), BF16),
                   jax.ShapeDtypeStruct((N, LANES), F32)],
        compiler_params=_params("arbitrary"),
        name="norm1_proj_attn",
    )(x2, g.reshape(1, D), mod3, w_bf)


def _mm_kernel(a_ref, w_ref, o_ref):
    o_ref[...] = _dot(a_ref[...], w_ref[...]).astype(o_ref.dtype)


def _matmul(a, w, out_dtype, tm, tn, name):
    m, k = a.shape
    n = w.shape[1]
    return pl.pallas_call(
        _mm_kernel,
        grid=(m // tm, n // tn),
        in_specs=[pl.BlockSpec((tm, k), lambda i, j: (i, 0)),
                  pl.BlockSpec((k, tn), lambda i, j: (0, j))],
        out_specs=pl.BlockSpec((tm, tn), lambda i, j: (i, j)),
        out_shape=jax.ShapeDtypeStruct((m, n), out_dtype),
        compiler_params=_params("arbitrary", "arbitrary"),
        name=name,
    )(a, w)


A_COLS = 2 * ATTN_W + 2 * KV_W + LANES


def _spread_matrix(groups):
    r = lax.broadcasted_iota(jnp.int32, (groups * HEAD_DIM, groups * 4 * HEAD_DIM), 0)
    c = lax.broadcasted_iota(jnp.int32, (groups * HEAD_DIM, groups * 4 * HEAD_DIM), 1)
    d, cc = r & (HEAD_DIM - 1), c & (4 * HEAD_DIM - 1)
    hit = jnp.logical_and(jnp.right_shift(r, 6) == jnp.right_shift(c, 8),
                          jnp.logical_or(cc == d, cc == d + 3 * HEAD_DIM))
    return jnp.where(hit, 1.0, 0.0).astype(BF16)


def _attn_kernel(q_ref, iq_lo_ref, iq_hi_ref, tail_ref, ik_ref, k_ref, v_ref, o_ref,
                 s_ref, thr_ref, tie_ref, qs_ref, iqs_ref, ikp_ref, kp_ref, vaug_ref, acc_ref,
                 lg0_ref, a0_ref, m0_ref, lg1_ref, a1_ref, m1_ref):
    qi = pl.program_id(1)
    nkc = qi // (TK // TQ) + 1

    @pl.when(qi == 0)
    def _():
        spread_k = _spread_matrix(N_KV)
        spread_ik = _spread_matrix(2)[0:LANES, 0:2 * LANES]
        lane_v = lax.broadcasted_iota(jnp.int32, (TK, LANES), 1)
        ones = jnp.ones((TK, LANES), BF16)

        def fill(c, carry):
            off = pl.multiple_of(c * TK, TK)
            kp_ref[pl.ds(off, TK), :] = _dot(k_ref[pl.ds(off, TK), :], spread_k).astype(BF16)
            ikp_ref[pl.ds(off, TK), :] = _dot(ik_ref[pl.ds(off, TK), :], spread_ik).astype(BF16)
            for g in range(N_KV):
                src = v_ref[pl.ds(off, TK), LANES * (g // 2):LANES * (g // 2 + 1)]
                keep = (lane_v < 64) if g % 2 == 0 else (lane_v >= 64)
                vaug_ref[pl.ds(off, TK), LANES * g:LANES * (g + 1)] = jnp.where(keep, src, ones)
            return carry

        lax.fori_loop(0, S // TK, fill, 0)

    ident = jnp.where(lax.broadcasted_iota(jnp.int32, (TQ, LANES), 0)
                      == lax.broadcasted_iota(jnp.int32, (TQ, LANES), 1), 1.0, 0.0).astype(BF16)
    for j in range(8):
        iq_half = iq_lo_ref if j < 4 else iq_hi_ref
        iqs_ref[j * TQ:(j + 1) * TQ, :] = iq_half[:, LANES * (j % 4):LANES * (j % 4 + 1)]
        qs_ref[j * TQ:(j + 1) * TQ, 0:LANES] = q_ref[:, LANES * j:LANES * (j + 1)] * (HEAD_DIM ** -0.5)
        qs_ref[j * TQ:(j + 1) * TQ, LANES:2 * LANES] = ident

    w = tail_ref[...].T[IDX_DIM:IDX_DIM + IDX_HEADS, :] * ((IDX_HEADS ** -0.5) * (IDX_DIM ** -0.5))

    keyi = lax.broadcasted_iota(jnp.int32, (TK, TQ), 0)
    qryi = qi * TQ + lax.broadcasted_iota(jnp.int32, (TK, TQ), 1)

    def score_body(c, carry):
        lo, hi = carry
        off = pl.multiple_of(c * TK, TK)
        ik_even = ikp_ref[pl.ds(off, TK), 0:LANES]
        ik_odd = ikp_ref[pl.ds(off, TK), LANES:2 * LANES]
        acc = jnp.zeros((TK, TQ), F32)
        for jj in range(4):
            iq2 = iqs_ref[jj * 2 * TQ:(jj + 1) * 2 * TQ, :]
            for par, ik_par in ((0, ik_even), (1, ik_odd)):
                d = _dot_nt(ik_par, iq2)
                for t in range(2):
                    h = 2 * (2 * jj + t) + par
                    acc = acc + jnp.maximum(d[:, t * TQ:(t + 1) * TQ], 0.0) * w[h:h + 1, :]
        valid = (off + keyi) <= qryi
        s_ref[pl.ds(off, TK), :] = jnp.where(valid, acc, -jnp.inf)
        hi = jnp.maximum(hi, jnp.max(jnp.where(valid, acc, -jnp.inf), axis=0, keepdims=True))
        lo = jnp.minimum(lo, jnp.min(jnp.where(valid, acc, jnp.inf), axis=0, keepdims=True))
        return lo, hi

    lo0, hi0 = lax.fori_loop(0, nkc, score_body,
                             (jnp.full((1, TQ), jnp.inf, F32), jnp.full((1, TQ), -jnp.inf, F32)))

    def count_keys(pred):
        def body(c, cnt):
            off = pl.multiple_of(c * TK, TK)
            hit = jnp.where(pred(s_ref[pl.ds(off, TK), :], off + keyi), 1.0, 0.0)
            return cnt + hit.reshape(TK // CNT_ROWS, CNT_ROWS, TQ).sum(axis=0)

        cnt = lax.fori_loop(0, nkc, body, jnp.zeros((CNT_ROWS, TQ), F32))
        return jnp.sum(cnt, axis=0, keepdims=True)

    def count_ge(mid):
        return count_keys(lambda sc, _: sc >= mid)

    tie_ref[...] = jnp.full((8, TQ), S, jnp.int32)

    @pl.when(qi * TQ + TQ <= TOPK)
    def _():
        thr_ref[...] = jnp.full((8, TQ), jnp.finfo(F32).min, F32)

    @pl.when(qi * TQ + TQ > TOPK)
    def _():
        kf = float(TOPK)

        def step(lo, hi):
            mid = lo + (hi - lo) * 0.5
            cnt = count_ge(mid)
            ge = cnt >= kf
            eq = cnt == kf
            lo2 = jnp.where(ge, mid, lo)
            hi2 = jnp.where(eq, mid, jnp.where(ge, hi, mid))
            stuck = jnp.logical_or(mid <= lo, mid >= hi)
            return lo2, hi2, jnp.logical_and(lo2 < hi2, jnp.logical_not(stuck))

        def cond(st):
            it, _, _, active = st
            return jnp.logical_and(it < BISECT_MAX_IT, active > 0.0)

        def body(st):
            it, lo, hi, _ = st
            lo, hi, _ = step(lo, hi)
            lo, hi, act = step(lo, hi)
            return it + 2, lo, hi, jnp.max(jnp.where(act, 1.0, 0.0))

        _, lo, _, _ = lax.while_loop(cond, body, (jnp.int32(0), lo0, hi0, jnp.float32(1.0)))
        thr_ref[...] = jnp.broadcast_to(lo, (8, TQ))

        need = count_ge(lo) > kf

        @pl.when(jnp.max(jnp.where(need, 1.0, 0.0)) > 0.0)
        def _():
            keep = kf - count_keys(lambda sc, _: sc > lo)

            def index_step(_, st):
                below, upto = st
                mid = jnp.right_shift(below + upto, 1)
                ok = count_keys(lambda sc, ki: jnp.logical_and(sc == lo, ki <= mid)) >= keep
                return jnp.where(ok, below, mid), jnp.where(ok, mid, upto)

            _, upto = lax.fori_loop(0, TIE_STEPS, index_step,
                                    (jnp.full((1, TQ), -1, jnp.int32), jnp.full((1, TQ), S - 1, jnp.int32)))
            tie_ref[...] = jnp.broadcast_to(jnp.where(need, upto, S), (8, TQ))

    acc_ref[...] = jnp.zeros(acc_ref.shape, F32)
    thr = thr_ref[0:1, :]
    tie = tie_ref[0:1, :]
    nt = TK // LANES
    n_blk = 2 * N_KV
    bufs = ((lg0_ref, a0_ref, m0_ref), (lg1_ref, a1_ref, m1_ref))
    m1_ref[...] = jnp.full(m1_ref.shape, MASKED, F32)

    def pass_a(c, slot):
        lg_ref, a_ref, m_ref = bufs[slot]
        m_prev_ref = bufs[1 - slot][2]
        off = pl.multiple_of(c * TK, TK)
        sc = s_ref[pl.ds(off, TK), :]
        sel = jnp.logical_or(sc > thr, jnp.logical_and(sc == thr, (off + keyi) <= tie))
        bias_t = jnp.where(sel, 0.0, MASKED).astype(BF16)
        for blk in range(n_blk):
            g, par = blk // 2, blk % 2
            qg = qs_ref[g * 2 * TQ:(g + 1) * 2 * TQ, :]
            kk = kp_ref[pl.ds(off, TK), g * 256 + par * LANES:g * 256 + (par + 1) * LANES]
            lg = _dot_nt(qg, jnp.concatenate([kk, bias_t], axis=1))
            lg_ref[blk] = lg
            m_old = m_prev_ref[blk]
            m_new = jnp.maximum(m_old, jnp.max(lg, axis=1, keepdims=True))
            a_ref[blk] = jnp.exp(m_old - m_new)
            m_ref[blk] = m_new

    def pass_b(c, slot):
        lg_ref, a_ref, m_ref = bufs[slot]
        off = pl.multiple_of(c * TK, TK)
        for blk in range(n_blk):
            g = blk // 2
            m_new = m_ref[blk]
            p = jnp.concatenate([jnp.exp(lg_ref[blk, :, LANES * t:LANES * (t + 1)] - m_new)
                                 for t in range(nt)], axis=1).astype(BF16)
            va = vaug_ref[pl.ds(off, TK), LANES * g:LANES * (g + 1)]
            acc_ref[blk] = a_ref[blk] * acc_ref[blk] + _dot(p, va)

    pass_a(0, 0)

    def pair_body(i, carry):
        c = 2 * i
        pass_a(c + 1, 1)
        pass_b(c, 0)
        pass_a(c + 2, 0)
        pass_b(c + 1, 1)
        return carry

    lax.fori_loop(0, (nkc - 1) // 2, pair_body, 0)

    @pl.when(nkc % 2 == 0)
    def _():
        pass_a(nkc - 1, 1)
        pass_b(nkc - 2, 0)
        pass_b(nkc - 1, 1)

    @pl.when(nkc % 2 == 1)
    def _():
        pass_b(nkc - 1, 0)

    lane = lax.broadcasted_iota(jnp.int32, (TQ, LANES), 1)
    for g in range(N_KV):
        acc_e = acc_ref[2 * g]
        acc_o = acc_ref[2 * g + 1]
        o_even = acc_e / pltpu.roll(acc_e, 64, axis=1)
        o_odd = acc_o / pltpu.roll(acc_o, 64, axis=1)
        for pr in range(2):
            xa = o_even[pr * TQ:(pr + 1) * TQ]
            xb = o_odd[pr * TQ:(pr + 1) * TQ]
            if g % 2 == 0:
                left, right = xa, pltpu.roll(xb, 64, axis=1)
            else:
                left, right = pltpu.roll(xa, 64, axis=1), xb
            col = (2 * g + pr) * LANES
            o_ref[:, col:col + LANES] = jnp.where(lane < 64, left, right).astype(o_ref.dtype)


def _attention(a, tail):
    nq = S // TQ
    rowmap = lambda b, i: b * nq + i
    return pl.pallas_call(
        _attn_kernel,
        grid=(B, nq),
        in_specs=[pl.BlockSpec((TQ, ATTN_W), lambda b, i: (rowmap(b, i), 0)),
                  pl.BlockSpec((TQ, ATTN_W // 2), lambda b, i: (rowmap(b, i), 3)),
                  pl.BlockSpec((TQ, ATTN_W // 2), lambda b, i: (rowmap(b, i), 4)),
                  pl.BlockSpec((TQ, LANES), lambda b, i: (rowmap(b, i), 0)),
                  pl.BlockSpec((S, LANES), lambda b, i: (b, 20)),
                  pl.BlockSpec((S, KV_W), lambda b, i: (b, 4)),
                  pl.BlockSpec((S, KV_W), lambda b, i: (b, 5))],
        out_specs=pl.BlockSpec((TQ, ATTN_W), lambda b, i: (rowmap(b, i), 0)),
        out_shape=jax.ShapeDtypeStruct((N, ATTN_W), BF16),
        scratch_shapes=[pltpu.VMEM((S, TQ), F32),
                        pltpu.VMEM((8, TQ), F32),
                        pltpu.VMEM((8, TQ), jnp.int32),
                        pltpu.VMEM((8 * TQ, 2 * LANES), BF16),
                        pltpu.VMEM((8 * TQ, LANES), BF16),
                        pltpu.VMEM((S, 2 * LANES), BF16),
                        pltpu.VMEM((S, N_KV * 2 * LANES), BF16),
                        pltpu.VMEM((S, N_KV * LANES), BF16),
                        pltpu.VMEM((2 * N_KV, 2 * TQ, LANES), F32)]
                       + 2 * [pltpu.VMEM((2 * N_KV, 2 * TQ, TK), F32),
                              pltpu.VMEM((2 * N_KV, 2 * TQ, LANES), F32),
                              pltpu.VMEM((2 * N_KV, 2 * TQ, LANES), F32)],

        compiler_params=_params("arbitrary", "arbitrary"),
        name="sparse_attn",
    )(a, a, a, tail, a, a, a)


CONV_TS = 512


def _conv_kernel(bg_ref, cg_ref, xc_ref, cw_ref, o_ref, carry_ref):
    i = pl.program_id(0)

    @pl.when(i % (S // CONV_TS) == 0)
    def _():
        carry_ref[...] = jnp.zeros(carry_ref.shape, F32)

    u = cg_ref[...] * xc_ref[...]
    prev = carry_ref[...]
    row = lax.broadcasted_iota(jnp.int32, u.shape, 0)
    u1 = jnp.where(row == 0, prev[7:8, :], pltpu.roll(u, 1, axis=0))
    u2 = jnp.where(row == 0, prev[6:7, :], jnp.where(row == 1, prev[7:8, :], pltpu.roll(u, 2, axis=0)))
    y = cw_ref[0:1, :] * u2 + cw_ref[1:2, :] * u1 + cw_ref[2:3, :] * u
    o_ref[...] = (bg_ref[...] * y).astype(o_ref.dtype)
    carry_ref[...] = u[CONV_TS - 8:CONV_TS, :]


def _short_conv(pb, conv_w):
    return pl.pallas_call(
        _conv_kernel,
        grid=(N // CONV_TS,),
        in_specs=[pl.BlockSpec((CONV_TS, CONV_W), lambda i: (i, 0)),
                  pl.BlockSpec((CONV_TS, CONV_W), lambda i: (i, 1)),
                  pl.BlockSpec((CONV_TS, CONV_W), lambda i: (i, 2)),
                  pl.BlockSpec((3, CONV_W), lambda i: (0, 0))],
        out_specs=pl.BlockSpec((CONV_TS, CONV_W), lambda i: (i, 0)),
        out_shape=jax.ShapeDtypeStruct((N, CONV_W), BF16),
        scratch_shapes=[pltpu.VMEM((8, CONV_W), F32)],
        compiler_params=_params("arbitrary"),
        name="short_conv",
    )(pb, pb, pb, conv_w)


MG_TM = 1024
MG_TN = 512


def _merge_kernel(oa_ref, zb_ref, wa_ref, wb_ref, ga_ref, gb_ref, o_ref):
    ya = _dot(oa_ref[...], wa_ref[...])
    yb = _dot(zb_ref[...], wb_ref[...])
    merged = jax.nn.sigmoid(ga_ref[...]) * ya + jax.nn.sigmoid(gb_ref[...]) * yb
    o_ref[...] = merged.astype(o_ref.dtype)


def _merge(oa, zb, wa, wb, pb):
    ga0 = (3 * CONV_W) // MG_TN
    gb0 = (3 * CONV_W + D) // MG_TN
    return pl.pallas_call(
        _merge_kernel,
        grid=(N // MG_TM, D // MG_TN),
        in_specs=[pl.BlockSpec((MG_TM, ATTN_W), lambda i, j: (i, 0)),
                  pl.BlockSpec((MG_TM, CONV_W), lambda i, j: (i, 0)),
                  pl.BlockSpec((ATTN_W, MG_TN), lambda i, j: (0, j)),
                  pl.BlockSpec((CONV_W, MG_TN), lambda i, j: (0, j)),
                  pl.BlockSpec((MG_TM, MG_TN), lambda i, j: (i, ga0 + j)),
                  pl.BlockSpec((MG_TM, MG_TN), lambda i, j: (i, gb0 + j))],
        out_specs=pl.BlockSpec((MG_TM, MG_TN), lambda i, j: (i, j)),
        out_shape=jax.ShapeDtypeStruct((N, D), BF16),
        compiler_params=_params("arbitrary", "arbitrary"),
        name="gated_merge",
    )(oa, zb, wa, wb, pb, pb)


def _wo_kernel(m_ref, w_ref, x_ref, mod_ref, o_ref):
    o_ref[...] = x_ref[...] + mod_ref[2:3, :] * _dot(m_ref[...], w_ref[...])


WO_TN = 1024


def _out_proj(merged, wo, x2, mod3):
    tpb = S // MG_TM
    return pl.pallas_call(
        _wo_kernel,
        grid=(N // MG_TM, D // WO_TN),
        in_specs=[pl.BlockSpec((MG_TM, D), lambda i, j: (i, 0)),
                  pl.BlockSpec((D, WO_TN), lambda i, j: (0, j)),
                  pl.BlockSpec((MG_TM, WO_TN), lambda i, j: (i, j)),
                  pl.BlockSpec((None, N_MOD, WO_TN), lambda i, j: (i // tpb, 0, j))],
        out_specs=pl.BlockSpec((MG_TM, WO_TN), lambda i, j: (i, j)),
        out_shape=jax.ShapeDtypeStruct((N, D), F32),
        compiler_params=_params("arbitrary", "arbitrary"),
        name="out_proj",
    )(merged, wo, x2, mod3)


RT_TM = 256


def _router_kernel(x_ref, g_ref, mod_ref, wr_ref, br_ref,
                   h_ref, mi_ref, mw_ref, cnt_ref, carry_ref):
    i = pl.program_id(0)

    @pl.when(i == 0)
    def _():
        carry_ref[...] = jnp.zeros(carry_ref.shape, F32)

    h = _modulated_norm(x_ref[...], g_ref[...], mod_ref[3:4, :], mod_ref[4:5, :])
    h_ref[...] = h
    lg = _dot(h.astype(BF16), wr_ref[...]) + br_ref[...]
    lane = lax.broadcasted_iota(jnp.int32, lg.shape, 1)
    ninf = -jnp.inf

    gl = jnp.where(lane < N_GROUPS, lg, ninf)
    gm = jnp.max(gl, axis=1, keepdims=True)
    g_sel = jnp.min(jnp.where(gl == gm, lane, LANES), axis=1, keepdims=True)
    p_g = 1.0 / jnp.sum(jnp.exp(gl - gm), axis=1, keepdims=True)

    grp = jnp.right_shift(lane - N_GROUPS, 3)
    in_grp = jnp.logical_and(jnp.logical_and(lane >= N_GROUPS, lane < N_GROUPS + N_EXP), grp == g_sel)
    el = jnp.where(in_grp, lg, ninf)
    m1 = jnp.max(el, axis=1, keepdims=True)
    i1 = jnp.min(jnp.where(el == m1, lane, LANES), axis=1, keepdims=True)
    el2 = jnp.where(lane == i1, ninf, el)
    m2 = jnp.max(el2, axis=1, keepdims=True)
    i2 = jnp.min(jnp.where(el2 == m2, lane, LANES), axis=1, keepdims=True)
    e = jnp.exp(m2 - m1)
    w1 = p_g * (1.0 / (1.0 + e))
    w2 = p_g * (e / (1.0 + e))
    e1 = i1 - N_GROUPS
    e2 = i2 - N_GROUPS

    oh1 = lane == e1
    oh2 = lane == e2
    used = jnp.where(jnp.logical_or(oh1, oh2), 1.0, 0.0)
    ri = lax.broadcasted_iota(jnp.int32, (RT_TM, RT_TM), 0)
    ci = lax.broadcasted_iota(jnp.int32, (RT_TM, RT_TM), 1)
    lower = jnp.where(ci < ri, 1.0, 0.0).astype(BF16)
    before = _dot(lower, used.astype(BF16)) + carry_ref[...]
    r1 = jnp.sum(jnp.where(oh1, before, 0.0), axis=1, keepdims=True).astype(jnp.int32)
    r2 = jnp.sum(jnp.where(oh2, before, 0.0), axis=1, keepdims=True).astype(jnp.int32)
    total = carry_ref[...] + jnp.sum(used, axis=0, keepdims=True)
    carry_ref[...] = total
    cnt_ref[...] = total

    zero = jnp.zeros_like(lane)
    mi_ref[...] = jnp.where(lane == 0, e1, jnp.where(lane == 1, e2,
                            jnp.where(lane == 2, r1, jnp.where(lane == 3, r2, zero))))
    mw_ref[...] = jnp.where(lane == 0, w1, jnp.where(lane == 1, w2, 0.0))


def _router(x1, g, mod3, wr, br):
    tpb = S // RT_TM
    return pl.pallas_call(
        _router_kernel,
        grid=(N // RT_TM,),
        in_specs=[pl.BlockSpec((RT_TM, D), lambda i: (i, 0)),
                  pl.BlockSpec((1, D), lambda i: (0, 0)),
                  pl.BlockSpec((None, N_MOD, D), lambda i: (i // tpb, 0, 0)),
                  pl.BlockSpec((D, LANES), lambda i: (0, 0)),
                  pl.BlockSpec((1, LANES), lambda i: (0, 0))],
        out_specs=[pl.BlockSpec((RT_TM, D), lambda i: (i, 0)),
                   pl.BlockSpec((RT_TM, LANES), lambda i: (i, 0)),
                   pl.BlockSpec((RT_TM, LANES), lambda i: (i, 0)),
                   pl.BlockSpec((1, LANES), lambda i: (0, 0))],
        out_shape=[jax.ShapeDtypeStruct((N, D), F32),
                   jax.ShapeDtypeStruct((N, LANES), jnp.int32),
                   jax.ShapeDtypeStruct((N, LANES), F32),
                   jax.ShapeDtypeStruct((1, LANES), F32)],
        scratch_shapes=[pltpu.VMEM((1, LANES), F32)],
        compiler_params=_params("arbitrary"),
        name="norm2_router",
    )(x1, g.reshape(1, D), mod3, wr, br)


DP_TM = 256


def _row_copy(src_ref, src_row, dst_ref, dst_row, sem):
    return pltpu.make_async_copy(src_ref.at[pl.ds(src_row, 1), :], dst_ref.at[pl.ds(dst_row, 1), :], sem)


def _dispatch_kernel(slot_ref, h_ref, xs_in_ref, xs_ref, sem):
    del xs_in_ref
    base = pl.program_id(0) * DP_TM

    for r in range(DP_TM):
        for k in range(2):
            _row_copy(h_ref, r, xs_ref, slot_ref[(base + r) * 2 + k], sem).start()

    def drain(r, carry):
        for _ in range(2):
            _row_copy(h_ref, 0, xs_ref, 0, sem).wait()
        return carry

    lax.fori_loop(0, DP_TM, drain, 0, unroll=8)


def _dispatch(slot, h2, xs0):
    grid_spec = pltpu.PrefetchScalarGridSpec(
        num_scalar_prefetch=1,
        grid=(N // DP_TM,),
        in_specs=[pl.BlockSpec((DP_TM, D), lambda i, *_: (i, 0)),
                  pl.BlockSpec(memory_space=pl.ANY)],
        out_specs=pl.BlockSpec(memory_space=pl.ANY),
        scratch_shapes=[pltpu.SemaphoreType.DMA(())],
    )
    return pl.pallas_call(
        _dispatch_kernel,
        grid_spec=grid_spec,
        out_shape=jax.ShapeDtypeStruct((XS_ROWS, D), F32),
        input_output_aliases={2: 0},
        compiler_params=_params("arbitrary"),
        name="moe_dispatch",
    )(slot, h2, xs0)


def _expert_kernel(ce_ref, nc_ref, x_ref, wg_ref, wu_ref, wd_ref, y_ref, wg_s, wu_s, wd_s):
    c = pl.program_id(0)
    valid = c < nc_ref[0]
    new_expert = jnp.logical_or(c == 0, ce_ref[c] != ce_ref[jnp.maximum(c - 1, 0)])

    @pl.when(jnp.logical_and(valid, new_expert))
    def _():
        wg_s[...] = wg_ref[...].astype(BF16)
        wu_s[...] = wu_ref[...].astype(BF16)
        wd_s[...] = wd_ref[...].astype(BF16)

    @pl.when(valid)
    def _():
        x = x_ref[...].astype(BF16)
        a = _dot(x, wg_s[...])
        u = _dot(x, wu_s[...])
        act = (a * jax.nn.sigmoid(a)) * u
        y_ref[...] = _dot(act.astype(BF16), wd_s[...])


def _experts(ce, nc, xs, wg, wu, wd):
    rows = lambda c, ce, nc: (jnp.minimum(c, nc[0] - 1), 0)
    wmap = lambda c, ce, nc: (ce[c], 0, 0)
    grid_spec = pltpu.PrefetchScalarGridSpec(
        num_scalar_prefetch=2,
        grid=(MAX_CHUNKS,),
        in_specs=[pl.BlockSpec((CH, D), rows),
                  pl.BlockSpec((None, D, FF), wmap),
                  pl.BlockSpec((None, D, FF), wmap),
                  pl.BlockSpec((None, FF, D), wmap)],
        out_specs=pl.BlockSpec((CH, D), rows),
        scratch_shapes=[pltpu.VMEM((D, FF), BF16), pltpu.VMEM((D, FF), BF16), pltpu.VMEM((FF, D), BF16)],
    )
    return pl.pallas_call(
        _expert_kernel,
        grid_spec=grid_spec,
        out_shape=jax.ShapeDtypeStruct((XS_ROWS, D), F32),
        input_output_aliases={2: 0},
        compiler_params=_params("arbitrary"),
        name="moe_experts",
    )(ce, nc, xs, wg, wu, wd)


CB_TM = 256


def _combine_kernel(slot_ref, y_ref, x_ref, mw_ref, mod_ref, g_ref, o_ref, ybuf, sem):
    i = pl.program_id(0)
    cur = i % 2

    def issue(tile, buf):
        for r in range(CB_TM):
            for k in range(2):
                _row_copy(y_ref, slot_ref[(tile * CB_TM + r) * 2 + k], ybuf.at[buf, k], r, sem.at[buf]).start()

    @pl.when(i == 0)
    def _():
        issue(0, 0)

    @pl.when(i + 1 < pl.num_programs(0))
    def _():
        issue(i + 1, 1 - cur)

    def drain(r, carry):
        for k in range(2):
            _row_copy(y_ref, 0, ybuf.at[cur, k], 0, sem.at[cur]).wait()
        return carry

    lax.fori_loop(0, CB_TM, drain, 0, unroll=8)

    moe = mw_ref[:, 0:1] * ybuf[cur, 0] + mw_ref[:, 1:2] * ybuf[cur, 1]
    x = x_ref[...] + mod_ref[5:6, :] * moe
    r = lax.rsqrt(jnp.mean(x * x, axis=-1, keepdims=True) + EPS)
    o_ref[...] = (x * r) * g_ref[...]


def _combine(slot, y, x1, mw, mod3, g):
    tpb = S // CB_TM
    grid_spec = pltpu.PrefetchScalarGridSpec(
        num_scalar_prefetch=1,
        grid=(N // CB_TM,),
        in_specs=[pl.BlockSpec(memory_space=pl.ANY),
                  pl.BlockSpec((CB_TM, D), lambda i, *_: (i, 0)),
                  pl.BlockSpec((CB_TM, LANES), lambda i, *_: (i, 0)),
                  pl.BlockSpec((None, N_MOD, D), lambda i, *_: (i // tpb, 0, 0)),
                  pl.BlockSpec((1, D), lambda i, *_: (0, 0))],
        out_specs=pl.BlockSpec((CB_TM, D), lambda i, *_: (i, 0)),
        scratch_shapes=[pltpu.VMEM((2, 2, CB_TM, D), F32), pltpu.SemaphoreType.DMA((2,))],
    )
    return pl.pallas_call(
        _combine_kernel,
        grid_spec=grid_spec,
        out_shape=jax.ShapeDtypeStruct((N, D), F32),
        compiler_params=_params("arbitrary"),
        name="moe_combine_norm",
    )(slot, y, x1, mw, mod3, g.reshape(1, D))


def kernel(x, c, w_ada, b_ada, g_norm_mix, w_in, conv_w, w_a_up, w_b_out, w_o, g_norm_ffn,
           w_rg, b_rg, w_re, b_re, w_e_gate, w_e_up, w_e_down, g_norm_final):
    x2 = x.reshape(N, D)
    mod3 = _ada(c, w_ada[0], b_ada[0]).reshape(B, N_MOD, D)

    o_bg = 2640
    w_bf = w_in.astype(BF16)
    w_b = w_bf[0, :, o_bg:]

    h, pa, tail = _norm_proj(x2, g_norm_mix[0], mod3, w_bf)
    pb = _matmul(h, w_b, F32, 1024, 1024, "proj_conv_gates")

    oa = _attention(pa, tail)
    zb = _short_conv(pb, conv_w[0])
    merged = _merge(oa, zb, w_a_up[0].astype(BF16), w_b_out[0].astype(BF16), pb)
    x1 = _out_proj(merged, w_o[0].astype(BF16), x2, mod3)

    wr = jnp.pad(jnp.concatenate([w_rg[0], w_re[0]], axis=1),
                 ((0, 0), (0, LANES - N_GROUPS - N_EXP))).astype(BF16)
    br = jnp.pad(jnp.concatenate([b_rg[0], b_re[0]]), (0, LANES - N_GROUPS - N_EXP)).reshape(1, LANES)
    h2, mi, mw, cnt = _router(x1, g_norm_ffn[0], mod3, wr, br)

    counts = cnt[0, :N_EXP].astype(jnp.int32)
    nch = (counts + CH - 1) // CH
    cum = jnp.cumsum(nch)
    offs = ((cum - nch) * CH).astype(jnp.int32)
    total = cum[-1]
    cidx = jnp.minimum(jnp.arange(MAX_CHUNKS, dtype=jnp.int32), total - 1)
    ce = jnp.minimum(jnp.sum((cum[None, :] <= cidx[:, None]).astype(jnp.int32), axis=1), N_EXP - 1)
    nc = total.reshape(1).astype(jnp.int32)
    seg = jnp.sum(jnp.where(mi[:, 0:2, None] == jnp.arange(N_EXP, dtype=jnp.int32), offs, 0), axis=-1)
    slot = (seg + mi[:, 2:4]).reshape(-1)

    xs = _dispatch(slot, h2, jnp.zeros((XS_ROWS, D), F32))
    y = _experts(ce, nc, xs, w_e_gate[0], w_e_up[0], w_e_down[0])
    out = _combine(slot, y, x1, mw, mod3, g_norm_final)
    return out.reshape(B, S, D)
```

```python
import functools

import jax
import jax.numpy as jnp
from jax import lax
from jax.experimental import pallas as pl
from jax.experimental.pallas import tpu as pltpu

F32 = jnp.float32
BF16 = jnp.bfloat16

D = 2048
B = 2
S = 4096
N = B * S
EPS = 1e-6
HEAD_DIM = 64
ATTN_W = 1024
N_KV = 4
KV_W = 256
IDX_HEADS = 16
IDX_DIM = 64
TOPK = min(256, S // 4)
CONV_W = 1024
N_GROUPS = 4
EPG = 8
N_EXP = 32
FF = 512
N_MOD = 6

LANES = 128
TQ = 128
TK = 512
CH = 576
MAX_CHUNKS = (N * 2) // CH + N_EXP
XS_ROWS = MAX_CHUNKS * CH
MASKED = -1e30
BISECT_MAX_IT = 320
TIE_STEPS = 13
CNT_ROWS = 64
VMEM_LIMIT = 56 * 1024 * 1024


def _dot(a, b):
    return jnp.dot(a, b, preferred_element_type=F32)


def _dot_nt(a, b):
    return lax.dot_general(a, b, (((1,), (1,)), ((), ())), preferred_element_type=F32)


def _params(*sem):
    return pltpu.CompilerParams(dimension_semantics=sem, vmem_limit_bytes=VMEM_LIMIT)


ADA_TN = 1024
ADA_KC = 256


def _ada_kernel(ct_ref, w_ref, b_ref, o_ref):
    tn = w_ref.shape[1]

    def body(i, accs):
        k0 = pl.multiple_of(i * ADA_KC, ADA_KC)
        ct = ct_ref[pl.ds(k0, ADA_KC), :]
        act = ct * jax.nn.sigmoid(ct)
        w = w_ref[pl.ds(k0, ADA_KC), :]
        out = []
        for b in range(B):
            prod = w * act[:, b:b + 1]
            out.append(accs[b] + prod.reshape(ADA_KC // 8, 8, tn).sum(axis=0))
        return tuple(out)

    accs = lax.fori_loop(0, D // ADA_KC, body, tuple(jnp.zeros((8, tn), F32) for _ in range(B)))
    rows = [jnp.sum(a, axis=0, keepdims=True) for a in accs]
    o_ref[...] = jnp.concatenate(rows, axis=0) + b_ref[...]


def _ada(c, w, b):
    n_out = w.shape[1]
    return pl.pallas_call(
        _ada_kernel,
        grid=(n_out // ADA_TN,),
        in_specs=[pl.BlockSpec((D, B), lambda j: (0, 0)),
                  pl.BlockSpec((D, ADA_TN), lambda j: (0, j)),
                  pl.BlockSpec((1, ADA_TN), lambda j: (0, j))],
        out_specs=pl.BlockSpec((B, ADA_TN), lambda j: (0, j)),
        out_shape=jax.ShapeDtypeStruct((B, n_out), F32),
        compiler_params=_params("arbitrary"),
        name="ada_mod",
    )(c.T, w, b.reshape(1, n_out))


NORM_TM = 512


def _modulated_norm(x, g, shift, scale):
    r = lax.rsqrt(jnp.mean(x * x, axis=-1, keepdims=True) + EPS)
    return (x * r) * g * (1.0 + scale) + shift


def _norm_proj_kernel(x_ref, g_ref, mod_ref, wa_ref, h_ref, pa_ref, tail_ref):
    h = _modulated_norm(x_ref[...], g_ref[...], mod_ref[0:1, :], mod_ref[1:2, :]).astype(BF16)
    h_ref[...] = h
    p = _dot(h, wa_ref[...])
    pa_ref[...] = p.astype(pa_ref.dtype)
    tail_ref[...] = p[:, A_COLS - LANES:]


def _norm_proj(x2, g, mod3, w_bf):
    tpb = S // NORM_TM
    n_a = A_COLS
    return pl.pallas_call(
        _norm_proj_kernel,
        grid=(N // NORM_TM,),
        in_specs=[pl.BlockSpec((NORM_TM, D), lambda i: (i, 0)),
                  pl.BlockSpec((1, D), lambda i: (0, 0)),
                  pl.BlockSpec((None, N_MOD, D), lambda i: (i // tpb, 0, 0)),
                  pl.BlockSpec((None, D, n_a), lambda i: (0, 0, 0))],
        out_specs=[pl.BlockSpec((NORM_TM, D), lambda i: (i, 0)),
                   pl.BlockSpec((NORM_TM, n_a), lambda i: (i, 0)),
                   pl.BlockSpec((NORM_TM, LANES), lambda i: (i, 0))],
        out_shape=[jax.ShapeDtypeStruct((N, D), BF16),
                   jax.ShapeDtypeStruct((N, n_a), BF16),
                   jax.ShapeDtypeStruct((N, LANES), F32)],
        compiler_params=_params("arbitrary"),
        name="norm1_proj_attn",
    )(x2, g.reshape(1, D), mod3, w_bf)


def _mm_kernel(a_ref, w_ref, o_ref):
    o_ref[...] = _dot(a_ref[...], w_ref[...]).astype(o_ref.dtype)


def _matmul(a, w, out_dtype, tm, tn, name):
    m, k = a.shape
    n = w.shape[1]
    return pl.pallas_call(
        _mm_kernel,
        grid=(m // tm, n // tn),
        in_specs=[pl.BlockSpec((tm, k), lambda i, j: (i, 0)),
                  pl.BlockSpec((k, tn), lambda i, j: (0, j))],
        out_specs=pl.BlockSpec((tm, tn), lambda i, j: (i, j)),
        out_shape=jax.ShapeDtypeStruct((m, n), out_dtype),
        compiler_params=_params("arbitrary", "arbitrary"),
        name=name,
    )(a, w)


A_COLS = 2 * ATTN_W + 2 * KV_W + LANES


def _spread_matrix(groups):
    r = lax.broadcasted_iota(jnp.int32, (groups * HEAD_DIM, groups * 4 * HEAD_DIM), 0)
    c = lax.broadcasted_iota(jnp.int32, (groups * HEAD_DIM, groups * 4 * HEAD_DIM), 1)
    d, cc = r & (HEAD_DIM - 1), c & (4 * HEAD_DIM - 1)
    hit = jnp.logical_and(jnp.right_shift(r, 6) == jnp.right_shift(c, 8),
                          jnp.logical_or(cc == d, cc == d + 3 * HEAD_DIM))
    return jnp.where(hit, 1.0, 0.0).astype(BF16)


def _attn_kernel(q_ref, iq_lo_ref, iq_hi_ref, tail_ref, ik_ref, k_ref, v_ref, o_ref,
                 s_ref, thr_ref, tie_ref, qs_ref, iqs_ref, ikp_ref, kp_ref, vaug_ref, acc_ref,
                 lg0_ref, a0_ref, m0_ref, lg1_ref, a1_ref, m1_ref):
    qi = pl.program_id(1)
    nkc = qi // (TK // TQ) + 1

    @pl.when(qi == 0)
    def _():
        spread_k = _spread_matrix(N_KV)
        spread_ik = _spread_matrix(2)[0:LANES, 0:2 * LANES]
        lane_v = lax.broadcasted_iota(jnp.int32, (TK, LANES), 1)
        ones = jnp.ones((TK, LANES), BF16)

        def fill(c, carry):
            off = pl.multiple_of(c * TK, TK)
            kp_ref[pl.ds(off, TK), :] = _dot(k_ref[pl.ds(off, TK), :], spread_k).astype(BF16)
            ikp_ref[pl.ds(off, TK), :] = _dot(ik_ref[pl.ds(off, TK), :], spread_ik).astype(BF16)
            for g in range(N_KV):
                src = v_ref[pl.ds(off, TK), LANES * (g // 2):LANES * (g // 2 + 1)]
                keep = (lane_v < 64) if g % 2 == 0 else (lane_v >= 64)
                vaug_ref[pl.ds(off, TK), LANES * g:LANES * (g + 1)] = jnp.where(keep, src, ones)
            return carry

        lax.fori_loop(0, S // TK, fill, 0)

    ident = jnp.where(lax.broadcasted_iota(jnp.int32, (TQ, LANES), 0)
                      == lax.broadcasted_iota(jnp.int32, (TQ, LANES), 1), 1.0, 0.0).astype(BF16)
    for j in range(8):
        iq_half = iq_lo_ref if j < 4 else iq_hi_ref
        iqs_ref[j * TQ:(j + 1) * TQ, :] = iq_half[:, LANES * (j % 4):LANES * (j % 4 + 1)]
        qs_ref[j * TQ:(j + 1) * TQ, 0:LANES] = q_ref[:, LANES * j:LANES * (j + 1)] * (HEAD_DIM ** -0.5)
        qs_ref[j * TQ:(j + 1) * TQ, LANES:2 * LANES] = ident

    w = tail_ref[...].T[IDX_DIM:IDX_DIM + IDX_HEADS, :] * ((IDX_HEADS ** -0.5) * (IDX_DIM ** -0.5))

    keyi = lax.broadcasted_iota(jnp.int32, (TK, TQ), 0)
    qryi = qi * TQ + lax.broadcasted_iota(jnp.int32, (TK, TQ), 1)

    def score_body(c, carry):
        lo, hi = carry
        off = pl.multiple_of(c * TK, TK)
        ik_even = ikp_ref[pl.ds(off, TK), 0:LANES]
        ik_odd = ikp_ref[pl.ds(off, TK), LANES:2 * LANES]
        acc = jnp.zeros((TK, TQ), F32)
        for jj in range(4):
            iq2 = iqs_ref[jj * 2 * TQ:(jj + 1) * 2 * TQ, :]
            for par, ik_par in ((0, ik_even), (1, ik_odd)):
                d = _dot_nt(ik_par, iq2)
                for t in range(2):
                    h = 2 * (2 * jj + t) + par
                    acc = acc + jnp.maximum(d[:, t * TQ:(t + 1) * TQ], 0.0) * w[h:h + 1, :]
        valid = (off + keyi) <= qryi
        s_ref[pl.ds(off, TK), :] = jnp.where(valid, acc, -jnp.inf)
        hi = jnp.maximum(hi, jnp.max(jnp.where(valid, acc, -jnp.inf), axis=0, keepdims=True))
        lo = jnp.minimum(lo, jnp.min(jnp.where(valid, acc, jnp.inf), axis=0, keepdims=True))
        return lo, hi

    lo0, hi0 = lax.fori_loop(0, nkc, score_body,
                             (jnp.full((1, TQ), jnp.inf, F32), jnp.full((1, TQ), -jnp.inf, F32)))

    def count_keys(pred):
        def body(c, cnt):
            off = pl.multiple_of(c * TK, TK)
            hit = jnp.where(pred(s_ref[pl.ds(off, TK), :], off + keyi), 1.0, 0.0)
            return cnt + hit.reshape(TK // CNT_ROWS, CNT_ROWS, TQ).sum(axis=0)

        cnt = lax.fori_loop(0, nkc, body, jnp.zeros((CNT_ROWS, TQ), F32))
        return jnp.sum(cnt, axis=0, keepdims=True)

    def count_ge(mid):
        return count_keys(lambda sc, _: sc >= mid)

    tie_ref[...] = jnp.full((8, TQ), S, jnp.int32)

    @pl.when(qi * TQ + TQ <= TOPK)
    def _():
        thr_ref[...] = jnp.full((8, TQ), jnp.finfo(F32).min, F32)

    @pl.when(qi * TQ + TQ > TOPK)
    def _():
        kf = float(TOPK)

        def step(lo, hi):
            mid = lo + (hi - lo) * 0.5
            cnt = count_ge(mid)
            ge = cnt >= kf
            eq = cnt == kf
            lo2 = jnp.where(ge, mid, lo)
            hi2 = jnp.where(eq, mid, jnp.where(ge, hi, mid))
            stuck = jnp.logical_or(mid <= lo, mid >= hi)
            return lo2, hi2, jnp.logical_and(lo2 < hi2, jnp.logical_not(stuck))

        def cond(st):
            it, _, _, active = st
            return jnp.logical_and(it < BISECT_MAX_IT, active > 0.0)

        def body(st):
            it, lo, hi, _ = st
            lo, hi, _ = step(lo, hi)
            lo, hi, act = step(lo, hi)
            return it + 2, lo, hi, jnp.max(jnp.where(act, 1.0, 0.0))

        _, lo, _, _ = lax.while_loop(cond, body, (jnp.int32(0), lo0, hi0, jnp.float32(1.0)))
        thr_ref[...] = jnp.broadcast_to(lo, (8, TQ))

        need = count_ge(lo) > kf

        @pl.when(jnp.max(jnp.where(need, 1.0, 0.0)) > 0.0)
        def _():
            keep = kf - count_keys(lambda sc, _: sc > lo)

            def index_step(_, st):
                below, upto = st
                mid = jnp.right_shift(below + upto, 1)
                ok = count_keys(lambda sc, ki: jnp.logical_and(sc == lo, ki <= mid)) >= keep
                return jnp.where(ok, below, mid), jnp.where(ok, mid, upto)

            _, upto = lax.fori_loop(0, TIE_STEPS, index_step,
                                    (jnp.full((1, TQ), -1, jnp.int32), jnp.full((1, TQ), S - 1, jnp.int32)))
            tie_ref[...] = jnp.broadcast_to(jnp.where(need, upto, S), (8, TQ))

    acc_ref[...] = jnp.zeros(acc_ref.shape, F32)
    thr = thr_ref[0:1, :]
    tie = tie_ref[0:1, :]
    nt = TK // LANES
    n_blk = 2 * N_KV
    bufs = ((lg0_ref, a0_ref, m0_ref), (lg1_ref, a1_ref, m1_ref))
    m1_ref[...] = jnp.full(m1_ref.shape, MASKED, F32)

    def pass_a(c, slot):
        lg_ref, a_ref, m_ref = bufs[slot]
        m_prev_ref = bufs[1 - slot][2]
        off = pl.multiple_of(c * TK, TK)
        sc = s_ref[pl.ds(off, TK), :]
        sel = jnp.logical_or(sc > thr, jnp.logical_and(sc == thr, (off + keyi) <= tie))
        bias_t = jnp.where(sel, 0.0, MASKED).astype(BF16)
        for blk in range(n_blk):
            g, par = blk // 2, blk % 2
            qg = qs_ref[g * 2 * TQ:(g + 1) * 2 * TQ, :]
            kk = kp_ref[pl.ds(off, TK), g * 256 + par * LANES:g * 256 + (par + 1) * LANES]
            lg = _dot_nt(qg, jnp.concatenate([kk, bias_t], axis=1))
            lg_ref[blk] = lg
            m_old = m_prev_ref[blk]
            m_new = jnp.maximum(m_old, jnp.max(lg, axis=1, keepdims=True))
            a_ref[blk] = jnp.exp(m_old - m_new)
            m_ref[blk] = m_new

    def pass_b(c, slot):
        lg_ref, a_ref, m_ref = bufs[slot]
        off = pl.multiple_of(c * TK, TK)
        for blk in range(n_blk):
            g = blk // 2
            m_new = m_ref[blk]
            p = jnp.concatenate([jnp.exp(lg_ref[blk, :, LANES * t:LANES * (t + 1)] - m_new)
                                 for t in range(nt)], axis=1).astype(BF16)
            va = vaug_ref[pl.ds(off, TK), LANES * g:LANES * (g + 1)]
            acc_ref[blk] = a_ref[blk] * acc_ref[blk] + _dot(p, va)

    pass_a(0, 0)

    def pair_body(i, carry):
        c = 2 * i
        pass_a(c + 1, 1)
        pass_b(c, 0)
        pass_a(c + 2, 0)
        pass_b(c + 1, 1)
        return carry

    lax.fori_loop(0, (nkc - 1) // 2, pair_body, 0)

    @pl.when(nkc % 2 == 0)
    def _():
        pass_a(nkc - 1, 1)
        pass_b(nkc - 2, 0)
        pass_b(nkc - 1, 1)

    @pl.when(nkc % 2 == 1)
    def _():
        pass_b(nkc - 1, 0)

    lane = lax.broadcasted_iota(jnp.int32, (TQ, LANES), 1)
    for g in range(N_KV):
        acc_e = acc_ref[2 * g]
        acc_o = acc_ref[2 * g + 1]
        o_even = acc_e / pltpu.roll(acc_e, 64, axis=1)
        o_odd = acc_o / pltpu.roll(acc_o, 64, axis=1)
        for pr in range(2):
            xa = o_even[pr * TQ:(pr + 1) * TQ]
            xb = o_odd[pr * TQ:(pr + 1) * TQ]
            if g % 2 == 0:
                left, right = xa, pltpu.roll(xb, 64, axis=1)
            else:
                left, right = pltpu.roll(xa, 64, axis=1), xb
            col = (2 * g + pr) * LANES
            o_ref[:, col:col + LANES] = jnp.where(lane < 64, left, right).astype(o_ref.dtype)


def _attention(a, tail):
    nq = S // TQ
    rowmap = lambda b, i: b * nq + i
    return pl.pallas_call(
        _attn_kernel,
        grid=(B, nq),
        in_specs=[pl.BlockSpec((TQ, ATTN_W), lambda b, i: (rowmap(b, i), 0)),
                  pl.BlockSpec((TQ, ATTN_W // 2), lambda b, i: (rowmap(b, i), 3)),
                  pl.BlockSpec((TQ, ATTN_W // 2), lambda b, i: (rowmap(b, i), 4)),
                  pl.BlockSpec((TQ, LANES), lambda b, i: (rowmap(b, i), 0)),
                  pl.BlockSpec((S, LANES), lambda b, i: (b, 20)),
                  pl.BlockSpec((S, KV_W), lambda b, i: (b, 4)),
                  pl.BlockSpec((S, KV_W), lambda b, i: (b, 5))],
        out_specs=pl.BlockSpec((TQ, ATTN_W), lambda b, i: (rowmap(b, i), 0)),
        out_shape=jax.ShapeDtypeStruct((N, ATTN_W), BF16),
        scratch_shapes=[pltpu.VMEM((S, TQ), F32),
                        pltpu.VMEM((8, TQ), F32),
                        pltpu.VMEM((8, TQ), jnp.int32),
                        pltpu.VMEM((8 * TQ, 2 * LANES), BF16),
                        pltpu.VMEM((8 * TQ, LANES), BF16),
                        pltpu.VMEM((S, 2 * LANES), BF16),
                        pltpu.VMEM((S, N_KV * 2 * LANES), BF16),
                        pltpu.VMEM((S, N_KV * LANES), BF16),
                        pltpu.VMEM((2 * N_KV, 2 * TQ, LANES), F32)]
                       + 2 * [pltpu.VMEM((2 * N_KV, 2 * TQ, TK), F32),
                              pltpu.VMEM((2 * N_KV, 2 * TQ, LANES), F32),
                              pltpu.VMEM((2 * N_KV, 2 * TQ, LANES), F32)],

        compiler_params=_params("arbitrary", "arbitrary"),
        name="sparse_attn",
    )(a, a, a, tail, a, a, a)


CONV_TS = 512


def _conv_kernel(bg_ref, cg_ref, xc_ref, cw_ref, o_ref, carry_ref):
    i = pl.program_id(0)

    @pl.when(i % (S // CONV_TS) == 0)
    def _():
        carry_ref[...] = jnp.zeros(carry_ref.shape, F32)

    u = cg_ref[...] * xc_ref[...]
    prev = carry_ref[...]
    row = lax.broadcasted_iota(jnp.int32, u.shape, 0)
    u1 = jnp.where(row == 0, prev[7:8, :], pltpu.roll(u, 1, axis=0))
    u2 = jnp.where(row == 0, prev[6:7, :], jnp.where(row == 1, prev[7:8, :], pltpu.roll(u, 2, axis=0)))
    y = cw_ref[0:1, :] * u2 + cw_ref[1:2, :] * u1 + cw_ref[2:3, :] * u
    o_ref[...] = (bg_ref[...] * y).astype(o_ref.dtype)
    carry_ref[...] = u[CONV_TS - 8:CONV_TS, :]


def _short_conv(pb, conv_w):
    return pl.pallas_call(
        _conv_kernel,
        grid=(N // CONV_TS,),
        in_specs=[pl.BlockSpec((CONV_TS, CONV_W), lambda i: (i, 0)),
                  pl.BlockSpec((CONV_TS, CONV_W), lambda i: (i, 1)),
                  pl.BlockSpec((CONV_TS, CONV_W), lambda i: (i, 2)),
                  pl.BlockSpec((3, CONV_W), lambda i: (0, 0))],
        out_specs=pl.BlockSpec((CONV_TS, CONV_W), lambda i: (i, 0)),
        out_shape=jax.ShapeDtypeStruct((N, CONV_W), BF16),
        scratch_shapes=[pltpu.VMEM((8, CONV_W), F32)],
        compiler_params=_params("arbitrary"),
        name="short_conv",
    )(pb, pb, pb, conv_w)


MG_TM = 1024
MG_TN = 512


def _merge_kernel(oa_ref, zb_ref, wa_ref, wb_ref, ga_ref, gb_ref, o_ref):
    ya = _dot(oa_ref[...], wa_ref[...])
    yb = _dot(zb_ref[...], wb_ref[...])
    merged = jax.nn.sigmoid(ga_ref[...]) * ya + jax.nn.sigmoid(gb_ref[...]) * yb
    o_ref[...] = merged.astype(o_ref.dtype)


def _merge(oa, zb, wa, wb, pb):
    ga0 = (3 * CONV_W) // MG_TN
    gb0 = (3 * CONV_W + D) // MG_TN
    return pl.pallas_call(
        _merge_kernel,
        grid=(N // MG_TM, D // MG_TN),
        in_specs=[pl.BlockSpec((MG_TM, ATTN_W), lambda i, j: (i, 0)),
                  pl.BlockSpec((MG_TM, CONV_W), lambda i, j: (i, 0)),
                  pl.BlockSpec((ATTN_W, MG_TN), lambda i, j: (0, j)),
                  pl.BlockSpec((CONV_W, MG_TN), lambda i, j: (0, j)),
                  pl.BlockSpec((MG_TM, MG_TN), lambda i, j: (i, ga0 + j)),
                  pl.BlockSpec((MG_TM, MG_TN), lambda i, j: (i, gb0 + j))],
        out_specs=pl.BlockSpec((MG_TM, MG_TN), lambda i, j: (i, j)),
        out_shape=jax.ShapeDtypeStruct((N, D), BF16),
        compiler_params=_params("arbitrary", "arbitrary"),
        name="gated_merge",
    )(oa, zb, wa, wb, pb, pb)


def _wo_kernel(m_ref, w_ref, x_ref, mod_ref, o_ref):
    o_ref[...] = x_ref[...] + mod_ref[2:3, :] * _dot(m_ref[...], w_ref[...])


WO_TN = 1024


def _out_proj(merged, wo, x2, mod3):
    tpb = S // MG_TM
    return pl.pallas_call(
        _wo_kernel,
        grid=(N // MG_TM, D // WO_TN),
        in_specs=[pl.BlockSpec((MG_TM, D), lambda i, j: (i, 0)),
                  pl.BlockSpec((D, WO_TN), lambda i, j: (0, j)),
                  pl.BlockSpec((MG_TM, WO_TN), lambda i, j: (i, j)),
                  pl.BlockSpec((None, N_MOD, WO_TN), lambda i, j: (i // tpb, 0, j))],
        out_specs=pl.BlockSpec((MG_TM, WO_TN), lambda i, j: (i, j)),
        out_shape=jax.ShapeDtypeStruct((N, D), F32),
        compiler_params=_params("arbitrary", "arbitrary"),
        name="out_proj",
    )(merged, wo, x2, mod3)


RT_TM = 256


def _router_kernel(x_ref, g_ref, mod_ref, wr_ref, br_ref,
                   h_ref, mi_ref, mw_ref, cnt_ref, carry_ref):
    i = pl.program_id(0)

    @pl.when(i == 0)
    def _():
        carry_ref[...] = jnp.zeros(carry_ref.shape, F32)

    h = _modulated_norm(x_ref[...], g_ref[...], mod_ref[3:4, :], mod_ref[4:5, :])
    h_ref[...] = h
    lg = _dot(h.astype(BF16), wr_ref[...]) + br_ref[...]
    lane = lax.broadcasted_iota(jnp.int32, lg.shape, 1)
    ninf = -jnp.inf

    gl = jnp.where(lane < N_GROUPS, lg, ninf)
    gm = jnp.max(gl, axis=1, keepdims=True)
    g_sel = jnp.min(jnp.where(gl == gm, lane, LANES), axis=1, keepdims=True)
    p_g = 1.0 / jnp.sum(jnp.exp(gl - gm), axis=1, keepdims=True)

    grp = jnp.right_shift(lane - N_GROUPS, 3)
    in_grp = jnp.logical_and(jnp.logical_and(lane >= N_GROUPS, lane < N_GROUPS + N_EXP), grp == g_sel)
    el = jnp.where(in_grp, lg, ninf)
    m1 = jnp.max(el, axis=1, keepdims=True)
    i1 = jnp.min(jnp.where(el == m1, lane, LANES), axis=1, keepdims=True)
    el2 = jnp.where(lane == i1, ninf, el)
    m2 = jnp.max(el2, axis=1, keepdims=True)
    i2 = jnp.min(jnp.where(el2 == m2, lane, LANES), axis=1, keepdims=True)
    e = jnp.exp(m2 - m1)
    w1 = p_g * (1.0 / (1.0 + e))
    w2 = p_g * (e / (1.0 + e))
    e1 = i1 - N_GROUPS
    e2 = i2 - N_GROUPS

    oh1 = lane == e1
    oh2 = lane == e2
    used = jnp.where(jnp.logical_or(oh1, oh2), 1.0, 0.0)
    ri = lax.broadcasted_iota(jnp.int32, (RT_TM, RT_TM), 0)
    ci = lax.broadcasted_iota(jnp.int32, (RT_TM, RT_TM), 1)
    lower = jnp.where(ci < ri, 1.0, 0.0).astype(BF16)
    before = _dot(lower, used.astype(BF16)) + carry_ref[...]
    r1 = jnp.sum(jnp.where(oh1, before, 0.0), axis=1, keepdims=True).astype(jnp.int32)
    r2 = jnp.sum(jnp.where(oh2, before, 0.0), axis=1, keepdims=True).astype(jnp.int32)
    total = carry_ref[...] + jnp.sum(used, axis=0, keepdims=True)
    carry_ref[...] = total
    cnt_ref[...] = total

    zero = jnp.zeros_like(lane)
    mi_ref[...] = jnp.where(lane == 0, e1, jnp.where(lane == 1, e2,
                            jnp.where(lane == 2, r1, jnp.where(lane == 3, r2, zero))))
    mw_ref[...] = jnp.where(lane == 0, w1, jnp.where(lane == 1, w2, 0.0))


def _router(x1, g, mod3, wr, br):
    tpb = S // RT_TM
    return pl.pallas_call(
        _router_kernel,
        grid=(N // RT_TM,),
        in_specs=[pl.BlockSpec((RT_TM, D), lambda i: (i, 0)),
                  pl.BlockSpec((1, D), lambda i: (0, 0)),
                  pl.BlockSpec((None, N_MOD, D), lambda i: (i // tpb, 0, 0)),
                  pl.BlockSpec((D, LANES), lambda i: (0, 0)),
                  pl.BlockSpec((1, LANES), lambda i: (0, 0))],
        out_specs=[pl.BlockSpec((RT_TM, D), lambda i: (i, 0)),
                   pl.BlockSpec((RT_TM, LANES), lambda i: (i, 0)),
                   pl.BlockSpec((RT_TM, LANES), lambda i: (i, 0)),
                   pl.BlockSpec((1, LANES), lambda i: (0, 0))],
        out_shape=[jax.ShapeDtypeStruct((N, D), F32),
                   jax.ShapeDtypeStruct((N, LANES), jnp.int32),
                   jax.ShapeDtypeStruct((N, LANES), F32),
                   jax.ShapeDtypeStruct((1, LANES), F32)],
        scratch_shapes=[pltpu.VMEM((1, LANES), F32)],
        compiler_params=_params("arbitrary"),
        name="norm2_router",
    )(x1, g.reshape(1, D), mod3, wr, br)


DP_TM = 256


def _row_copy(src_ref, src_row, dst_ref, dst_row, sem):
    return pltpu.make_async_copy(src_ref.at[pl.ds(src_row, 1), :], dst_ref.at[pl.ds(dst_row, 1), :], sem)


def _dispatch_kernel(slot_ref, h_ref, xs_in_ref, xs_ref, sem):
    del xs_in_ref
    base = pl.program_id(0) * DP_TM

    for r in range(DP_TM):
        for k in range(2):
            _row_copy(h_ref, r, xs_ref, slot_ref[(base + r) * 2 + k], sem).start(priority=k)

    def drain(r, carry):
        for _ in range(2):
            _row_copy(h_ref, 0, xs_ref, 0, sem).wait()
        return carry

    lax.fori_loop(0, DP_TM, drain, 0, unroll=8)


def _dispatch(slot, h2, xs0):
    grid_spec = pltpu.PrefetchScalarGridSpec(
        num_scalar_prefetch=1,
        grid=(N // DP_TM,),
        in_specs=[pl.BlockSpec((DP_TM, D), lambda i, *_: (i, 0)),
                  pl.BlockSpec(memory_space=pl.ANY)],
        out_specs=pl.BlockSpec(memory_space=pl.ANY),
        scratch_shapes=[pltpu.SemaphoreType.DMA(())],
    )
    return pl.pallas_call(
        _dispatch_kernel,
        grid_spec=grid_spec,
        out_shape=jax.ShapeDtypeStruct((XS_ROWS, D), F32),
        input_output_aliases={2: 0},
        compiler_params=_params("arbitrary"),
        name="moe_dispatch",
    )(slot, h2, xs0)


def _expert_kernel(ce_ref, nc_ref, x_ref, wg_ref, wu_ref, wd_ref, y_ref, wg_s, wu_s, wd_s):
    c = pl.program_id(0)
    valid = c < nc_ref[0]
    new_expert = jnp.logical_or(c == 0, ce_ref[c] != ce_ref[jnp.maximum(c - 1, 0)])

    @pl.when(jnp.logical_and(valid, new_expert))
    def _():
        wg_s[...] = wg_ref[...].astype(BF16)
        wu_s[...] = wu_ref[...].astype(BF16)
        wd_s[...] = wd_ref[...].astype(BF16)

    @pl.when(valid)
    def _():
        x = x_ref[...].astype(BF16)
        a = _dot(x, wg_s[...])
        u = _dot(x, wu_s[...])
        act = (a * jax.nn.sigmoid(a)) * u
        y_ref[...] = _dot(act.astype(BF16), wd_s[...])


def _experts(ce, nc, xs, wg, wu, wd):
    rows = lambda c, ce, nc: (jnp.minimum(c, nc[0] - 1), 0)
    wmap = lambda c, ce, nc: (ce[c], 0, 0)
    grid_spec = pltpu.PrefetchScalarGridSpec(
        num_scalar_prefetch=2,
        grid=(MAX_CHUNKS,),
        in_specs=[pl.BlockSpec((CH, D), rows),
                  pl.BlockSpec((None, D, FF), wmap),
                  pl.BlockSpec((None, D, FF), wmap),
                  pl.BlockSpec((None, FF, D), wmap)],
        out_specs=pl.BlockSpec((CH, D), rows),
        scratch_shapes=[pltpu.VMEM((D, FF), BF16), pltpu.VMEM((D, FF), BF16), pltpu.VMEM((FF, D), BF16)],
    )
    return pl.pallas_call(
        _expert_kernel,
        grid_spec=grid_spec,
        out_shape=jax.ShapeDtypeStruct((XS_ROWS, D), F32),
        input_output_aliases={2: 0},
        compiler_params=_params("arbitrary"),
        name="moe_experts",
    )(ce, nc, xs, wg, wu, wd)


CB_TM = 256


def _combine_kernel(slot_ref, y_ref, x_ref, mw_ref, mod_ref, g_ref, o_ref, ybuf, sem):
    i = pl.program_id(0)
    cur = i % 2

    def issue(tile, buf):
        for r in range(CB_TM):
            for k in range(2):
                _row_copy(y_ref, slot_ref[(tile * CB_TM + r) * 2 + k], ybuf.at[buf, k], r,
                          sem.at[buf]).start(priority=k)

    @pl.when(i == 0)
    def _():
        issue(0, 0)

    @pl.when(i + 1 < pl.num_programs(0))
    def _():
        issue(i + 1, 1 - cur)

    def drain(r, carry):
        for k in range(2):
            _row_copy(y_ref, 0, ybuf.at[cur, k], 0, sem.at[cur]).wait()
        return carry

    lax.fori_loop(0, CB_TM, drain, 0, unroll=8)

    moe = mw_ref[:, 0:1] * ybuf[cur, 0] + mw_ref[:, 1:2] * ybuf[cur, 1]
    x = x_ref[...] + mod_ref[5:6, :] * moe
    r = lax.rsqrt(jnp.mean(x * x, axis=-1, keepdims=True) + EPS)
    o_ref[...] = (x * r) * g_ref[...]


def _combine(slot, y, x1, mw, mod3, g):
    tpb = S // CB_TM
    grid_spec = pltpu.PrefetchScalarGridSpec(
        num_scalar_prefetch=1,
        grid=(N // CB_TM,),
        in_specs=[pl.BlockSpec(memory_space=pl.ANY),
                  pl.BlockSpec((CB_TM, D), lambda i, *_: (i, 0)),
                  pl.BlockSpec((CB_TM, LANES), lambda i, *_: (i, 0)),
                  pl.BlockSpec((None, N_MOD, D), lambda i, *_: (i // tpb, 0, 0)),
                  pl.BlockSpec((1, D), lambda i, *_: (0, 0))],
        out_specs=pl.BlockSpec((CB_TM, D), lambda i, *_: (i, 0)),
        scratch_shapes=[pltpu.VMEM((2, 2, CB_TM, D), F32), pltpu.SemaphoreType.DMA((2,))],
    )
    return pl.pallas_call(
        _combine_kernel,
        grid_spec=grid_spec,
        out_shape=jax.ShapeDtypeStruct((N, D), F32),
        compiler_params=_params("arbitrary"),
        name="moe_combine_norm",
    )(slot, y, x1, mw, mod3, g.reshape(1, D))


def kernel(x, c, w_ada, b_ada, g_norm_mix, w_in, conv_w, w_a_up, w_b_out, w_o, g_norm_ffn,
           w_rg, b_rg, w_re, b_re, w_e_gate, w_e_up, w_e_down, g_norm_final):
    x2 = x.reshape(N, D)
    mod3 = _ada(c, w_ada[0], b_ada[0]).reshape(B, N_MOD, D)

    o_bg = 2640
    w_bf = w_in.astype(BF16)
    w_b = w_bf[0, :, o_bg:]

    h, pa, tail = _norm_proj(x2, g_norm_mix[0], mod3, w_bf)
    pb = _matmul(h, w_b, F32, 1024, 1024, "proj_conv_gates")

    oa = _attention(pa, tail)
    zb = _short_conv(pb, conv_w[0])
    merged = _merge(oa, zb, w_a_up[0].astype(BF16), w_b_out[0].astype(BF16), pb)
    x1 = _out_proj(merged, w_o[0].astype(BF16), x2, mod3)

    wr = jnp.pad(jnp.concatenate([w_rg[0], w_re[0]], axis=1),
                 ((0, 0), (0, LANES - N_GROUPS - N_EXP))).astype(BF16)
    br = jnp.pad(jnp.concatenate([b_rg[0], b_re[0]]), (0, LANES - N_GROUPS - N_EXP)).reshape(1, LANES)
    h2, mi, mw, cnt = _router(x1, g_norm_ffn[0], mod3, wr, br)

    counts = cnt[0, :N_EXP].astype(jnp.int32)
    nch = (counts + CH - 1) // CH
    cum = jnp.cumsum(nch)
    offs = ((cum - nch) * CH).astype(jnp.int32)
    total = cum[-1]
    cidx = jnp.minimum(jnp.arange(MAX_CHUNKS, dtype=jnp.int32), total - 1)
    ce = jnp.minimum(jnp.sum((cum[None, :] <= cidx[:, None]).astype(jnp.int32), axis=1), N_EXP - 1)
    nc = total.reshape(1).astype(jnp.int32)
    seg = jnp.sum(jnp.where(mi[:, 0:2, None] == jnp.arange(N_EXP, dtype=jnp.int32), offs, 0), axis=-1)
    slot = (seg + mi[:, 2:4]).reshape(-1)

    xs = _dispatch(slot, h2, jnp.zeros((XS_ROWS, D), F32))
    y = _experts(ce, nc, xs, w_e_gate[0], w_e_up[0], w_e_down[0])
    out = _combine(slot, y, x1, mw, mod3, g_norm_final)
    return out.reshape(B, S, D)
```

```python
import functools

import jax
import jax.numpy as jnp
from jax import lax
from jax.experimental import pallas as pl
from jax.experimental.pallas import tpu as pltpu

F32 = jnp.float32
BF16 = jnp.bfloat16

D = 2048
B = 2
S = 4096
N = B * S
EPS = 1e-6
HEAD_DIM = 64
ATTN_W = 1024
N_KV = 4
KV_W = 256
IDX_HEADS = 16
IDX_DIM = 64
TOPK = min(256, S // 4)
CONV_W = 1024
N_GROUPS = 4
EPG = 8
N_EXP = 32
FF = 512
N_MOD = 6

LANES = 128
TQ = 128
TK = 512
CH = 576
MAX_CHUNKS = (N * 2) // CH + N_EXP
XS_ROWS = MAX_CHUNKS * CH
MASKED = -1e30
BISECT_MAX_IT = 320
TIE_STEPS = 13
CNT_ROWS = 64
VMEM_LIMIT = 56 * 1024 * 1024


def _dot(a, b):
    return jnp.dot(a, b, preferred_element_type=F32)


def _dot_nt(a, b):
    return lax.dot_general(a, b, (((1,), (1,)), ((), ())), preferred_element_type=F32)


def _params(*sem):
    return pltpu.CompilerParams(dimension_semantics=sem, vmem_limit_bytes=VMEM_LIMIT)


ADA_TN = 1024
ADA_KC = 256


def _ada_kernel(ct_ref, w_ref, b_ref, o_ref):
    tn = w_ref.shape[1]

    def body(i, accs):
        k0 = pl.multiple_of(i * ADA_KC, ADA_KC)
        ct = ct_ref[pl.ds(k0, ADA_KC), :]
        act = ct * jax.nn.sigmoid(ct)
        w = w_ref[pl.ds(k0, ADA_KC), :]
        out = []
        for b in range(B):
            prod = w * act[:, b:b + 1]
            out.append(accs[b] + prod.reshape(ADA_KC // 8, 8, tn).sum(axis=0))
        return tuple(out)

    accs = lax.fori_loop(0, D // ADA_KC, body, tuple(jnp.zeros((8, tn), F32) for _ in range(B)))
    rows = [jnp.sum(a, axis=0, keepdims=True) for a in accs]
    o_ref[...] = jnp.concatenate(rows, axis=0) + b_ref[...]


def _ada(c, w, b):
    n_out = w.shape[1]
    return pl.pallas_call(
        _ada_kernel,
        grid=(n_out // ADA_TN,),
        in_specs=[pl.BlockSpec((D, B), lambda j: (0, 0)),
                  pl.BlockSpec((D, ADA_TN), lambda j: (0, j)),
                  pl.BlockSpec((1, ADA_TN), lambda j: (0, j))],
        out_specs=pl.BlockSpec((B, ADA_TN), lambda j: (0, j)),
        out_shape=jax.ShapeDtypeStruct((B, n_out), F32),
        compiler_params=_params("arbitrary"),
        name="ada_mod",
    )(c.T, w, b.reshape(1, n_out))


NORM_TM = 512


def _modulated_norm(x, g, shift, scale):
    r = lax.rsqrt(jnp.mean(x * x, axis=-1, keepdims=True) + EPS)
    return (x * r) * g * (1.0 + scale) + shift


def _norm_proj_kernel(x_ref, g_ref, mod_ref, wa_ref, h_ref, pa_ref, tail_ref):
    h = _modulated_norm(x_ref[...], g_ref[...], mod_ref[0:1, :], mod_ref[1:2, :]).astype(BF16)
    h_ref[...] = h
    p = _dot(h, wa_ref[...])
    pa_ref[...] = p.astype(pa_ref.dtype)
    tail_ref[...] = p[:, A_COLS - LANES:]


def _norm_proj(x2, g, mod3, w_bf):
    tpb = S // NORM_TM
    n_a = A_COLS
    return pl.pallas_call(
        _norm_proj_kernel,
        grid=(N // NORM_TM,),
        in_specs=[pl.BlockSpec((NORM_TM, D), lambda i: (i, 0)),
                  pl.BlockSpec((1, D), lambda i: (0, 0)),
                  pl.BlockSpec((None, N_MOD, D), lambda i: (i // tpb, 0, 0)),
                  pl.BlockSpec((None, D, n_a), lambda i: (0, 0, 0))],
        out_specs=[pl.BlockSpec((NORM_TM, D), lambda i: (i, 0)),
                   pl.BlockSpec((NORM_TM, n_a), lambda i: (i, 0)),
                   pl.BlockSpec((NORM_TM, LANES), lambda i: (i, 0))],
        out_shape=[jax.ShapeDtypeStruct((N, D), BF16),
                   jax.ShapeDtypeStruct((N, n_a), BF16),
                   jax.ShapeDtypeStruct((N, LANES), F32)],
        compiler_params=_params("arbitrary"),
        name="norm1_proj_attn",
    )(x2, g.reshape(1, D), mod3, w_bf)


def _mm_kernel(a_ref, w_ref, o_ref):
    o_ref[...] = _dot(a_ref[...], w_ref[...]).astype(o_ref.dtype)


def _matmul(a, w, out_dtype, tm, tn, name):
    m, k = a.shape
    n = w.shape[1]
    return pl.pallas_call(
        _mm_kernel,
        grid=(m // tm, n // tn),
        in_specs=[pl.BlockSpec((tm, k), lambda i, j: (i, 0)),
                  pl.BlockSpec((k, tn), lambda i, j: (0, j))],
        out_specs=pl.BlockSpec((tm, tn), lambda i, j: (i, j)),
        out_shape=jax.ShapeDtypeStruct((m, n), out_dtype),
        compiler_params=_params("arbitrary", "arbitrary"),
        name=name,
    )(a, w)


A_COLS = 2 * ATTN_W + 2 * KV_W + LANES


def _spread_matrix(groups):
    r = lax.broadcasted_iota(jnp.int32, (groups * HEAD_DIM, groups * 4 * HEAD_DIM), 0)
    c = lax.broadcasted_iota(jnp.int32, (groups * HEAD_DIM, groups * 4 * HEAD_DIM), 1)
    d, cc = r & (HEAD_DIM - 1), c & (4 * HEAD_DIM - 1)
    hit = jnp.logical_and(jnp.right_shift(r, 6) == jnp.right_shift(c, 8),
                          jnp.logical_or(cc == d, cc == d + 3 * HEAD_DIM))
    return jnp.where(hit, 1.0, 0.0).astype(BF16)


def _attn_kernel(q_ref, iq_lo_ref, iq_hi_ref, tail_ref, ik_ref, k_ref, v_ref, o_ref,
                 s_ref, thr_ref, tie_ref, qs_ref, iqs_ref, ikp_ref, kp_ref, vaug_ref, acc_ref,
                 lg0_ref, a0_ref, m0_ref, lg1_ref, a1_ref, m1_ref):
    qi = pl.program_id(1)
    nkc = qi // (TK // TQ) + 1

    @pl.when(qi == 0)
    def _():
        spread_k = _spread_matrix(N_KV)
        spread_ik = _spread_matrix(2)[0:LANES, 0:2 * LANES]
        lane_v = lax.broadcasted_iota(jnp.int32, (TK, LANES), 1)
        ones = jnp.ones((TK, LANES), BF16)

        def fill(c, carry):
            off = pl.multiple_of(c * TK, TK)
            kp_ref[pl.ds(off, TK), :] = _dot(k_ref[pl.ds(off, TK), :], spread_k).astype(BF16)
            ikp_ref[pl.ds(off, TK), :] = _dot(ik_ref[pl.ds(off, TK), :], spread_ik).astype(BF16)
            for g in range(N_KV):
                src = v_ref[pl.ds(off, TK), LANES * (g // 2):LANES * (g // 2 + 1)]
                keep = (lane_v < 64) if g % 2 == 0 else (lane_v >= 64)
                vaug_ref[pl.ds(off, TK), LANES * g:LANES * (g + 1)] = jnp.where(keep, src, ones)
            return carry

        lax.fori_loop(0, S // TK, fill, 0)

    ident = jnp.where(lax.broadcasted_iota(jnp.int32, (TQ, LANES), 0)
                      == lax.broadcasted_iota(jnp.int32, (TQ, LANES), 1), 1.0, 0.0).astype(BF16)
    for j in range(8):
        iq_half = iq_lo_ref if j < 4 else iq_hi_ref
        iqs_ref[j * TQ:(j + 1) * TQ, :] = iq_half[:, LANES * (j % 4):LANES * (j % 4 + 1)]
        qs_ref[j * TQ:(j + 1) * TQ, 0:LANES] = q_ref[:, LANES * j:LANES * (j + 1)] * (HEAD_DIM ** -0.5)
        qs_ref[j * TQ:(j + 1) * TQ, LANES:2 * LANES] = ident

    w = tail_ref[...].T[IDX_DIM:IDX_DIM + IDX_HEADS, :] * ((IDX_HEADS ** -0.5) * (IDX_DIM ** -0.5))

    keyi = lax.broadcasted_iota(jnp.int32, (TK, TQ), 0)
    qryi = qi * TQ + lax.broadcasted_iota(jnp.int32, (TK, TQ), 1)

    def score_body(c, carry):
        lo, hi = carry
        off = pl.multiple_of(c * TK, TK)
        ik_even = ikp_ref[pl.ds(off, TK), 0:LANES]
        ik_odd = ikp_ref[pl.ds(off, TK), LANES:2 * LANES]
        acc = jnp.zeros((TK, TQ), F32)
        for jj in range(4):
            iq2 = iqs_ref[jj * 2 * TQ:(jj + 1) * 2 * TQ, :]
            for par, ik_par in ((0, ik_even), (1, ik_odd)):
                d = _dot_nt(ik_par, iq2)
                for t in range(2):
                    h = 2 * (2 * jj + t) + par
                    acc = acc + jnp.maximum(d[:, t * TQ:(t + 1) * TQ], 0.0) * w[h:h + 1, :]
        valid = (off + keyi) <= qryi
        s_ref[pl.ds(off, TK), :] = jnp.where(valid, acc, -jnp.inf)
        hi = jnp.maximum(hi, jnp.max(jnp.where(valid, acc, -jnp.inf), axis=0, keepdims=True))
        lo = jnp.minimum(lo, jnp.min(jnp.where(valid, acc, jnp.inf), axis=0, keepdims=True))
        return lo, hi

    lo0, hi0 = lax.fori_loop(0, nkc, score_body,
                             (jnp.full((1, TQ), jnp.inf, F32), jnp.full((1, TQ), -jnp.inf, F32)))

    def count_keys(pred):
        def body(c, cnt):
            off = pl.multiple_of(c * TK, TK)
            hit = jnp.where(pred(s_ref[pl.ds(off, TK), :], off + keyi), 1.0, 0.0)
            return cnt + hit.reshape(TK // CNT_ROWS, CNT_ROWS, TQ).sum(axis=0)

        cnt = lax.fori_loop(0, nkc, body, jnp.zeros((CNT_ROWS, TQ), F32))
        return jnp.sum(cnt, axis=0, keepdims=True)

    def count_ge(mid):
        return count_keys(lambda sc, _: sc >= mid)

    tie_ref[...] = jnp.full((8, TQ), S, jnp.int32)

    @pl.when(qi * TQ + TQ <= TOPK)
    def _():
        thr_ref[...] = jnp.full((8, TQ), jnp.finfo(F32).min, F32)

    @pl.when(qi * TQ + TQ > TOPK)
    def _():
        kf = float(TOPK)

        def step(lo, hi):
            mid = lo + (hi - lo) * 0.5
            cnt = count_ge(mid)
            ge = cnt >= kf
            eq = cnt == kf
            lo2 = jnp.where(ge, mid, lo)
            hi2 = jnp.where(eq, mid, jnp.where(ge, hi, mid))
            stuck = jnp.logical_or(mid <= lo, mid >= hi)
            return lo2, hi2, jnp.logical_and(lo2 < hi2, jnp.logical_not(stuck))

        def cond(st):
            it, _, _, active = st
            return jnp.logical_and(it < BISECT_MAX_IT, active > 0.0)

        def body(st):
            it, lo, hi, _ = st
            lo, hi, _ = step(lo, hi)
            lo, hi, act = step(lo, hi)
            return it + 2, lo, hi, jnp.max(jnp.where(act, 1.0, 0.0))

        _, lo, _, _ = lax.while_loop(cond, body, (jnp.int32(0), lo0, hi0, jnp.float32(1.0)))
        thr_ref[...] = jnp.broadcast_to(lo, (8, TQ))

        need = count_ge(lo) > kf

        @pl.when(jnp.max(jnp.where(need, 1.0, 0.0)) > 0.0)
        def _():
            keep = kf - count_keys(lambda sc, _: sc > lo)

            def index_step(_, st):
                below, upto = st
                mid = jnp.right_shift(below + upto, 1)
                ok = count_keys(lambda sc, ki: jnp.logical_and(sc == lo, ki <= mid)) >= keep
                return jnp.where(ok, below, mid), jnp.where(ok, mid, upto)

            _, upto = lax.fori_loop(0, TIE_STEPS, index_step,
                                    (jnp.full((1, TQ), -1, jnp.int32), jnp.full((1, TQ), S - 1, jnp.int32)))
            tie_ref[...] = jnp.broadcast_to(jnp.where(need, upto, S), (8, TQ))

    acc_ref[...] = jnp.zeros(acc_ref.shape, F32)
    thr = thr_ref[0:1, :]
    tie = tie_ref[0:1, :]
    nt = TK // LANES
    n_blk = 2 * N_KV
    bufs = ((lg0_ref, a0_ref, m0_ref), (lg1_ref, a1_ref, m1_ref))
    m1_ref[...] = jnp.full(m1_ref.shape, MASKED, F32)

    def pass_a(c, slot):
        lg_ref, a_ref, m_ref = bufs[slot]
        m_prev_ref = bufs[1 - slot][2]
        off = pl.multiple_of(c * TK, TK)
        sc = s_ref[pl.ds(off, TK), :]
        sel = jnp.logical_or(sc > thr, jnp.logical_and(sc == thr, (off + keyi) <= tie))
        bias_t = jnp.where(sel, 0.0, MASKED).astype(BF16)
        for blk in range(n_blk):
            g, par = blk // 2, blk % 2
            qg = qs_ref[g * 2 * TQ:(g + 1) * 2 * TQ, :]
            kk = kp_ref[pl.ds(off, TK), g * 256 + par * LANES:g * 256 + (par + 1) * LANES]
            lg = _dot_nt(qg, jnp.concatenate([kk, bias_t], axis=1))
            lg_ref[blk] = lg
            m_old = m_prev_ref[blk]
            m_new = jnp.maximum(m_old, jnp.max(lg, axis=1, keepdims=True))
            a_ref[blk] = jnp.exp(m_old - m_new)
            m_ref[blk] = m_new

    def pass_b(c, slot):
        lg_ref, a_ref, m_ref = bufs[slot]
        off = pl.multiple_of(c * TK, TK)
        for blk in range(n_blk):
            g = blk // 2
            m_new = m_ref[blk]
            p = jnp.concatenate([jnp.exp(lg_ref[blk, :, LANES * t:LANES * (t + 1)] - m_new)
                                 for t in range(nt)], axis=1).astype(BF16)
            va = vaug_ref[pl.ds(off, TK), LANES * g:LANES * (g + 1)]
            acc_ref[blk] = a_ref[blk] * acc_ref[blk] + _dot(p, va)

    pass_a(0, 0)

    def pair_body(i, carry):
        c = 2 * i
        pass_a(c + 1, 1)
        pass_b(c, 0)
        pass_a(c + 2, 0)
        pass_b(c + 1, 1)
        return carry

    lax.fori_loop(0, (nkc - 1) // 2, pair_body, 0)

    @pl.when(nkc % 2 == 0)
    def _():
        pass_a(nkc - 1, 1)
        pass_b(nkc - 2, 0)
        pass_b(nkc - 1, 1)

    @pl.when(nkc % 2 == 1)
    def _():
        pass_b(nkc - 1, 0)

    lane = lax.broadcasted_iota(jnp.int32, (TQ, LANES), 1)
    for g in range(N_KV):
        acc_e = acc_ref[2 * g]
        acc_o = acc_ref[2 * g + 1]
        o_even = acc_e / pltpu.roll(acc_e, 64, axis=1)
        o_odd = acc_o / pltpu.roll(acc_o, 64, axis=1)
        for pr in range(2):
            xa = o_even[pr * TQ:(pr + 1) * TQ]
            xb = o_odd[pr * TQ:(pr + 1) * TQ]
            if g % 2 == 0:
                left, right = xa, pltpu.roll(xb, 64, axis=1)
            else:
                left, right = pltpu.roll(xa, 64, axis=1), xb
            col = (2 * g + pr) * LANES
            o_ref[:, col:col + LANES] = jnp.where(lane < 64, left, right).astype(o_ref.dtype)


def _attention(a, tail):
    nq = S // TQ
    rowmap = lambda b, i: b * nq + i
    return pl.pallas_call(
        _attn_kernel,
        grid=(B, nq),
        in_specs=[pl.BlockSpec((TQ, ATTN_W), lambda b, i: (rowmap(b, i), 0)),
                  pl.BlockSpec((TQ, ATTN_W // 2), lambda b, i: (rowmap(b, i), 3)),
                  pl.BlockSpec((TQ, ATTN_W // 2), lambda b, i: (rowmap(b, i), 4)),
                  pl.BlockSpec((TQ, LANES), lambda b, i: (rowmap(b, i), 0)),
                  pl.BlockSpec((S, LANES), lambda b, i: (b, 20)),
                  pl.BlockSpec((S, KV_W), lambda b, i: (b, 4)),
                  pl.BlockSpec((S, KV_W), lambda b, i: (b, 5))],
        out_specs=pl.BlockSpec((TQ, ATTN_W), lambda b, i: (rowmap(b, i), 0)),
        out_shape=jax.ShapeDtypeStruct((N, ATTN_W), BF16),
        scratch_shapes=[pltpu.VMEM((S, TQ), F32),
                        pltpu.VMEM((8, TQ), F32),
                        pltpu.VMEM((8, TQ), jnp.int32),
                        pltpu.VMEM((8 * TQ, 2 * LANES), BF16),
                        pltpu.VMEM((8 * TQ, LANES), BF16),
                        pltpu.VMEM((S, 2 * LANES), BF16),
                        pltpu.VMEM((S, N_KV * 2 * LANES), BF16),
                        pltpu.VMEM((S, N_KV * LANES), BF16),
                        pltpu.VMEM((2 * N_KV, 2 * TQ, LANES), F32)]
                       + 2 * [pltpu.VMEM((2 * N_KV, 2 * TQ, TK), F32),
                              pltpu.VMEM((2 * N_KV, 2 * TQ, LANES), F32),
                              pltpu.VMEM((2 * N_KV, 2 * TQ, LANES), F32)],

        compiler_params=_params("arbitrary", "arbitrary"),
        name="sparse_attn",
    )(a, a, a, tail, a, a, a)


CONV_TS = 512


def _conv_kernel(bg_ref, cg_ref, xc_ref, cw_ref, o_ref, carry_ref):
    i = pl.program_id(0)

    @pl.when(i % (S // CONV_TS) == 0)
    def _():
        carry_ref[...] = jnp.zeros(carry_ref.shape, F32)

    u = cg_ref[...] * xc_ref[...]
    prev = carry_ref[...]
    row = lax.broadcasted_iota(jnp.int32, u.shape, 0)
    u1 = jnp.where(row == 0, prev[7:8, :], pltpu.roll(u, 1, axis=0))
    u2 = jnp.where(row == 0, prev[6:7, :], jnp.where(row == 1, prev[7:8, :], pltpu.roll(u, 2, axis=0)))
    y = cw_ref[0:1, :] * u2 + cw_ref[1:2, :] * u1 + cw_ref[2:3, :] * u
    o_ref[...] = (bg_ref[...] * y).astype(o_ref.dtype)
    carry_ref[...] = u[CONV_TS - 8:CONV_TS, :]


def _short_conv(pb, conv_w):
    return pl.pallas_call(
        _conv_kernel,
        grid=(N // CONV_TS,),
        in_specs=[pl.BlockSpec((CONV_TS, CONV_W), lambda i: (i, 0)),
                  pl.BlockSpec((CONV_TS, CONV_W), lambda i: (i, 1)),
                  pl.BlockSpec((CONV_TS, CONV_W), lambda i: (i, 2)),
                  pl.BlockSpec((3, CONV_W), lambda i: (0, 0))],
        out_specs=pl.BlockSpec((CONV_TS, CONV_W), lambda i: (i, 0)),
        out_shape=jax.ShapeDtypeStruct((N, CONV_W), BF16),
        scratch_shapes=[pltpu.VMEM((8, CONV_W), F32)],
        compiler_params=_params("arbitrary"),
        name="short_conv",
    )(pb, pb, pb, conv_w)


MG_TM = 1024
MG_TN = 512


def _merge_kernel(oa_ref, zb_ref, wa_ref, wb_ref, ga_ref, gb_ref, o_ref):
    ya = _dot(oa_ref[...], wa_ref[...])
    yb = _dot(zb_ref[...], wb_ref[...])
    merged = jax.nn.sigmoid(ga_ref[...]) * ya + jax.nn.sigmoid(gb_ref[...]) * yb
    o_ref[...] = merged.astype(o_ref.dtype)


def _merge(oa, zb, wa, wb, pb):
    ga0 = (3 * CONV_W) // MG_TN
    gb0 = (3 * CONV_W + D) // MG_TN
    return pl.pallas_call(
        _merge_kernel,
        grid=(N // MG_TM, D // MG_TN),
        in_specs=[pl.BlockSpec((MG_TM, ATTN_W), lambda i, j: (i, 0)),
                  pl.BlockSpec((MG_TM, CONV_W), lambda i, j: (i, 0)),
                  pl.BlockSpec((ATTN_W, MG_TN), lambda i, j: (0, j)),
                  pl.BlockSpec((CONV_W, MG_TN), lambda i, j: (0, j)),
                  pl.BlockSpec((MG_TM, MG_TN), lambda i, j: (i, ga0 + j)),
                  pl.BlockSpec((MG_TM, MG_TN), lambda i, j: (i, gb0 + j))],
        out_specs=pl.BlockSpec((MG_TM, MG_TN), lambda i, j: (i, j)),
        out_shape=jax.ShapeDtypeStruct((N, D), BF16),
        compiler_params=_params("arbitrary", "arbitrary"),
        name="gated_merge",
    )(oa, zb, wa, wb, pb, pb)


def _wo_kernel(m_ref, w_ref, x_ref, mod_ref, o_ref):
    o_ref[...] = x_ref[...] + mod_ref[2:3, :] * _dot(m_ref[...], w_ref[...])


WO_TN = 1024


def _out_proj(merged, wo, x2, mod3):
    tpb = S // MG_TM
    return pl.pallas_call(
        _wo_kernel,
        grid=(N // MG_TM, D // WO_TN),
        in_specs=[pl.BlockSpec((MG_TM, D), lambda i, j: (i, 0)),
                  pl.BlockSpec((D, WO_TN), lambda i, j: (0, j)),
                  pl.BlockSpec((MG_TM, WO_TN), lambda i, j: (i, j)),
                  pl.BlockSpec((None, N_MOD, WO_TN), lambda i, j: (i // tpb, 0, j))],
        out_specs=pl.BlockSpec((MG_TM, WO_TN), lambda i, j: (i, j)),
        out_shape=jax.ShapeDtypeStruct((N, D), F32),
        compiler_params=_params("arbitrary", "arbitrary"),
        name="out_proj",
    )(merged, wo, x2, mod3)


RT_TM = 256


def _router_kernel(x_ref, g_ref, mod_ref, wr_ref, br_ref,
                   h_ref, mi_ref, mw_ref, cnt_ref, carry_ref):
    i = pl.program_id(0)

    @pl.when(i == 0)
    def _():
        carry_ref[...] = jnp.zeros(carry_ref.shape, F32)

    h = _modulated_norm(x_ref[...], g_ref[...], mod_ref[3:4, :], mod_ref[4:5, :])
    h_ref[...] = h
    lg = _dot(h.astype(BF16), wr_ref[...]) + br_ref[...]
    lane = lax.broadcasted_iota(jnp.int32, lg.shape, 1)
    ninf = -jnp.inf

    gl = jnp.where(lane < N_GROUPS, lg, ninf)
    gm = jnp.max(gl, axis=1, keepdims=True)
    g_sel = jnp.min(jnp.where(gl == gm, lane, LANES), axis=1, keepdims=True)
    p_g = 1.0 / jnp.sum(jnp.exp(gl - gm), axis=1, keepdims=True)

    grp = jnp.right_shift(lane - N_GROUPS, 3)
    in_grp = jnp.logical_and(jnp.logical_and(lane >= N_GROUPS, lane < N_GROUPS + N_EXP), grp == g_sel)
    el = jnp.where(in_grp, lg, ninf)
    m1 = jnp.max(el, axis=1, keepdims=True)
    i1 = jnp.min(jnp.where(el == m1, lane, LANES), axis=1, keepdims=True)
    el2 = jnp.where(lane == i1, ninf, el)
    m2 = jnp.max(el2, axis=1, keepdims=True)
    i2 = jnp.min(jnp.where(el2 == m2, lane, LANES), axis=1, keepdims=True)
    e = jnp.exp(m2 - m1)
    w1 = p_g * (1.0 / (1.0 + e))
    w2 = p_g * (e / (1.0 + e))
    e1 = i1 - N_GROUPS
    e2 = i2 - N_GROUPS

    oh1 = lane == e1
    oh2 = lane == e2
    used = jnp.where(jnp.logical_or(oh1, oh2), 1.0, 0.0)
    ri = lax.broadcasted_iota(jnp.int32, (RT_TM, RT_TM), 0)
    ci = lax.broadcasted_iota(jnp.int32, (RT_TM, RT_TM), 1)
    lower = jnp.where(ci < ri, 1.0, 0.0).astype(BF16)
    before = _dot(lower, used.astype(BF16)) + carry_ref[...]
    r1 = jnp.sum(jnp.where(oh1, before, 0.0), axis=1, keepdims=True).astype(jnp.int32)
    r2 = jnp.sum(jnp.where(oh2, before, 0.0), axis=1, keepdims=True).astype(jnp.int32)
    total = carry_ref[...] + jnp.sum(used, axis=0, keepdims=True)
    carry_ref[...] = total
    cnt_ref[...] = total

    zero = jnp.zeros_like(lane)
    mi_ref[...] = jnp.where(lane == 0, e1, jnp.where(lane == 1, e2,
                            jnp.where(lane == 2, r1, jnp.where(lane == 3, r2, zero))))
    mw_ref[...] = jnp.where(lane == 0, w1, jnp.where(lane == 1, w2, 0.0))


def _router(x1, g, mod3, wr, br):
    tpb = S // RT_TM
    return pl.pallas_call(
        _router_kernel,
        grid=(N // RT_TM,),
        in_specs=[pl.BlockSpec((RT_TM, D), lambda i: (i, 0)),
                  pl.BlockSpec((1, D), lambda i: (0, 0)),
                  pl.BlockSpec((None, N_MOD, D), lambda i: (i // tpb, 0, 0)),
                  pl.BlockSpec((D, LANES), lambda i: (0, 0)),
                  pl.BlockSpec((1, LANES), lambda i: (0, 0))],
        out_specs=[pl.BlockSpec((RT_TM, D), lambda i: (i, 0)),
                   pl.BlockSpec((RT_TM, LANES), lambda i: (i, 0)),
                   pl.BlockSpec((RT_TM, LANES), lambda i: (i, 0)),
                   pl.BlockSpec((1, LANES), lambda i: (0, 0))],
        out_shape=[jax.ShapeDtypeStruct((N, D), F32),
                   jax.ShapeDtypeStruct((N, LANES), jnp.int32),
                   jax.ShapeDtypeStruct((N, LANES), F32),
                   jax.ShapeDtypeStruct((1, LANES), F32)],
        scratch_shapes=[pltpu.VMEM((1, LANES), F32)],
        compiler_params=_params("arbitrary"),
        name="norm2_router",
    )(x1, g.reshape(1, D), mod3, wr, br)


DP_TM = 256


def _row_copy(src_ref, src_row, dst_ref, dst_row, sem):
    return pltpu.make_async_copy(src_ref.at[pl.ds(src_row, 1), :], dst_ref.at[pl.ds(dst_row, 1), :], sem)


def _dispatch_kernel(slot_ref, h_ref, xs_in_ref, xs_ref, sem):
    del xs_in_ref
    base = pl.program_id(0) * DP_TM

    for r in range(DP_TM):
        for k in range(2):
            _row_copy(h_ref, r, xs_ref, slot_ref[(base + r) * 2 + k], sem).start(priority=k)

    def drain(r, carry):
        for _ in range(2):
            _row_copy(h_ref, 0, xs_ref, 0, sem).wait()
        return carry

    lax.fori_loop(0, DP_TM, drain, 0, unroll=8)


def _dispatch(slot, h2, xs0):
    grid_spec = pltpu.PrefetchScalarGridSpec(
        num_scalar_prefetch=1,
        grid=(N // DP_TM,),
        in_specs=[pl.BlockSpec((DP_TM, D), lambda i, *_: (i, 0)),
                  pl.BlockSpec(memory_space=pl.ANY)],
        out_specs=pl.BlockSpec(memory_space=pl.ANY),
        scratch_shapes=[pltpu.SemaphoreType.DMA(())],
    )
    return pl.pallas_call(
        _dispatch_kernel,
        grid_spec=grid_spec,
        out_shape=jax.ShapeDtypeStruct((XS_ROWS, D), F32),
        input_output_aliases={2: 0},
        compiler_params=_params("arbitrary"),
        name="moe_dispatch",
    )(slot, h2, xs0)


def _expert_kernel(ce_ref, nc_ref, x_ref, wg_ref, wu_ref, wd_ref, y_ref, wg_s, wu_s, wd_s):
    c = pl.program_id(0)
    valid = c < nc_ref[0]
    new_expert = jnp.logical_or(c == 0, ce_ref[c] != ce_ref[jnp.maximum(c - 1, 0)])

    @pl.when(jnp.logical_and(valid, new_expert))
    def _():
        wg_s[...] = wg_ref[...].astype(BF16)
        wu_s[...] = wu_ref[...].astype(BF16)
        wd_s[...] = wd_ref[...].astype(BF16)

    @pl.when(valid)
    def _():
        x = x_ref[...].astype(BF16)
        a = _dot(x, wg_s[...])
        u = _dot(x, wu_s[...])
        act = (a * jax.nn.sigmoid(a)) * u
        y_ref[...] = _dot(act.astype(BF16), wd_s[...])


def _experts(ce, nc, xs, wg, wu, wd):
    rows = lambda c, ce, nc: (jnp.minimum(c, nc[0] - 1), 0)
    wmap = lambda c, ce, nc: (ce[c], 0, 0)
    grid_spec = pltpu.PrefetchScalarGridSpec(
        num_scalar_prefetch=2,
        grid=(MAX_CHUNKS,),
        in_specs=[pl.BlockSpec((CH, D), rows),
                  pl.BlockSpec((None, D, FF), wmap),
                  pl.BlockSpec((None, D, FF), wmap),
                  pl.BlockSpec((None, FF, D), wmap)],
        out_specs=pl.BlockSpec((CH, D), rows),
        scratch_shapes=[pltpu.VMEM((D, FF), BF16), pltpu.VMEM((D, FF), BF16), pltpu.VMEM((FF, D), BF16)],
    )
    return pl.pallas_call(
        _expert_kernel,
        grid_spec=grid_spec,
        out_shape=jax.ShapeDtypeStruct((XS_ROWS, D), F32),
        input_output_aliases={2: 0},
        compiler_params=_params("arbitrary"),
        name="moe_experts",
    )(ce, nc, xs, wg, wu, wd)


CB_TM = 256


def _combine_kernel(slot_ref, y_ref, x_ref, mw_ref, mod_ref, g_ref, o_ref, ybuf, sem):
    i = pl.program_id(0)
    cur = i % 2

    def issue(tile, buf):
        for r in range(CB_TM):
            for k in range(2):
                _row_copy(y_ref, slot_ref[(tile * CB_TM + r) * 2 + k], ybuf.at[buf, k], r,
                          sem.at[buf]).start(priority=k)

    @pl.when(i == 0)
    def _():
        issue(0, 0)

    @pl.when(i + 1 < pl.num_programs(0))
    def _():
        issue(i + 1, 1 - cur)

    def drain(r, carry):
        for k in range(2):
            _row_copy(y_ref, 0, ybuf.at[cur, k], 0, sem.at[cur]).wait()
        return carry

    lax.fori_loop(0, CB_TM, drain, 0, unroll=8)

    moe = mw_ref[:, 0:1] * ybuf[cur, 0] + mw_ref[:, 1:2] * ybuf[cur, 1]
    x = x_ref[...] + mod_ref[5:6, :] * moe
    r = lax.rsqrt(jnp.mean(x * x, axis=-1, keepdims=True) + EPS)
    o_ref[...] = (x * r) * g_ref[...]


def _combine(slot, y, x1, mw, mod3, g):
    tpb = S // CB_TM
    grid_spec = pltpu.PrefetchScalarGridSpec(
        num_scalar_prefetch=1,
        grid=(N // CB_TM,),
        in_specs=[pl.BlockSpec(memory_space=pl.ANY),
                  pl.BlockSpec((CB_TM, D), lambda i, *_: (i, 0)),
                  pl.BlockSpec((CB_TM, LANES), lambda i, *_: (i, 0)),
                  pl.BlockSpec((None, N_MOD, D), lambda i, *_: (i // tpb, 0, 0)),
                  pl.BlockSpec((1, D), lambda i, *_: (0, 0))],
        out_specs=pl.BlockSpec((CB_TM, D), lambda i, *_: (i, 0)),
        scratch_shapes=[pltpu.VMEM((2, 2, CB_TM, D), F32), pltpu.SemaphoreType.DMA((2,))],
    )
    return pl.pallas_call(
        _combine_kernel,
        grid_spec=grid_spec,
        out_shape=jax.ShapeDtypeStruct((N, D), F32),
        compiler_params=_params("arbitrary"),
        name="moe_combine_norm",
    )(slot, y, x1, mw, mod3, g.reshape(1, D))


def kernel(x, c, w_ada, b_ada, g_norm_mix, w_in, conv_w, w_a_up, w_b_out, w_o, g_norm_ffn,
           w_rg, b_rg, w_re, b_re, w_e_gate, w_e_up, w_e_down, g_norm_final):
    x2 = x.reshape(N, D)
    mod3 = _ada(c, w_ada[0], b_ada[0]).reshape(B, N_MOD, D)

    o_bg = 2640
    w_bf = w_in.astype(BF16)
    w_b = w_bf[0, :, o_bg:]

    h, pa, tail = _norm_proj(x2, g_norm_mix[0], mod3, w_bf)
    pb = _matmul(h, w_b, F32, 2048, 1024, "proj_conv_gates")

    oa = _attention(pa, tail)
    zb = _short_conv(pb, conv_w[0])
    merged = _merge(oa, zb, w_a_up[0].astype(BF16), w_b_out[0].astype(BF16), pb)
    x1 = _out_proj(merged, w_o[0].astype(BF16), x2, mod3)

    wr = jnp.pad(jnp.concatenate([w_rg[0], w_re[0]], axis=1),
                 ((0, 0), (0, LANES - N_GROUPS - N_EXP))).astype(BF16)
    br = jnp.pad(jnp.concatenate([b_rg[0], b_re[0]]), (0, LANES - N_GROUPS - N_EXP)).reshape(1, LANES)
    h2, mi, mw, cnt = _router(x1, g_norm_ffn[0], mod3, wr, br)

    counts = cnt[0, :N_EXP].astype(jnp.int32)
    nch = (counts + CH - 1) // CH
    cum = jnp.cumsum(nch)
    offs = ((cum - nch) * CH).astype(jnp.int32)
    total = cum[-1]
    cidx = jnp.minimum(jnp.arange(MAX_CHUNKS, dtype=jnp.int32), total - 1)
    ce = jnp.minimum(jnp.sum((cum[None, :] <= cidx[:, None]).astype(jnp.int32), axis=1), N_EXP - 1)
    nc = total.reshape(1).astype(jnp.int32)
    seg = jnp.sum(jnp.where(mi[:, 0:2, None] == jnp.arange(N_EXP, dtype=jnp.int32), offs, 0), axis=-1)
    slot = (seg + mi[:, 2:4]).reshape(-1)

    xs = _dispatch(slot, h2, jnp.zeros((XS_ROWS, D), F32))
    y = _experts(ce, nc, xs, w_e_gate[0], w_e_up[0], w_e_down[0])
    out = _combine(slot, y, x1, mw, mod3, g_norm_final)
    return out.reshape(B, S, D)
```

```python
import functools

import jax
import jax.numpy as jnp
from jax import lax
from jax.experimental import pallas as pl
from jax.experimental.pallas import tpu as pltpu

F32 = jnp.float32
BF16 = jnp.bfloat16

D = 2048
B = 2
S = 4096
N = B * S
EPS = 1e-6
HEAD_DIM = 64
ATTN_W = 1024
N_KV = 4
KV_W = 256
IDX_HEADS = 16
IDX_DIM = 64
TOPK = min(256, S // 4)
CONV_W = 1024
N_GROUPS = 4
EPG = 8
N_EXP = 32
FF = 512
N_MOD = 6

LANES = 128
TQ = 128
TK = 512
CH = 576
MAX_CHUNKS = (N * 2) // CH + N_EXP
XS_ROWS = MAX_CHUNKS * CH
MASKED = -1e30
BISECT_MAX_IT = 320
TIE_STEPS = 13
CNT_ROWS = 64
VMEM_LIMIT = 56 * 1024 * 1024


def _dot(a, b):
    return jnp.dot(a, b, preferred_element_type=F32)


def _dot_nt(a, b):
    return lax.dot_general(a, b, (((1,), (1,)), ((), ())), preferred_element_type=F32)


def _params(*sem):
    return pltpu.CompilerParams(dimension_semantics=sem, vmem_limit_bytes=VMEM_LIMIT)


ADA_TN = 1024
ADA_KC = 256


def _ada_kernel(ct_ref, w_ref, b_ref, o_ref):
    tn = w_ref.shape[1]

    def body(i, accs):
        k0 = pl.multiple_of(i * ADA_KC, ADA_KC)
        ct = ct_ref[pl.ds(k0, ADA_KC), :]
        act = ct * jax.nn.sigmoid(ct)
        w = w_ref[pl.ds(k0, ADA_KC), :]
        out = []
        for b in range(B):
            prod = w * act[:, b:b + 1]
            out.append(accs[b] + prod.reshape(ADA_KC // 8, 8, tn).sum(axis=0))
        return tuple(out)

    accs = lax.fori_loop(0, D // ADA_KC, body, tuple(jnp.zeros((8, tn), F32) for _ in range(B)))
    rows = [jnp.sum(a, axis=0, keepdims=True) for a in accs]
    o_ref[...] = jnp.concatenate(rows, axis=0) + b_ref[...]


def _ada(c, w, b):
    n_out = w.shape[1]
    return pl.pallas_call(
        _ada_kernel,
        grid=(n_out // ADA_TN,),
        in_specs=[pl.BlockSpec((D, B), lambda j: (0, 0)),
                  pl.BlockSpec((D, ADA_TN), lambda j: (0, j)),
                  pl.BlockSpec((1, ADA_TN), lambda j: (0, j))],
        out_specs=pl.BlockSpec((B, ADA_TN), lambda j: (0, j)),
        out_shape=jax.ShapeDtypeStruct((B, n_out), F32),
        compiler_params=_params("arbitrary"),
        name="ada_mod",
    )(c.T, w, b.reshape(1, n_out))


NORM_TM = 512


def _modulated_norm(x, g, shift, scale):
    r = lax.rsqrt(jnp.mean(x * x, axis=-1, keepdims=True) + EPS)
    return (x * r) * g * (1.0 + scale) + shift


def _norm_proj_kernel(x_ref, g_ref, mod_ref, wa_ref, h_ref, pa_ref, tail_ref):
    h = _modulated_norm(x_ref[...], g_ref[...], mod_ref[0:1, :], mod_ref[1:2, :]).astype(BF16)
    h_ref[...] = h
    p = _dot(h, wa_ref[...])
    pa_ref[...] = p.astype(pa_ref.dtype)
    tail_ref[...] = p[:, A_COLS - LANES:]


def _norm_proj(x2, g, mod3, w_bf):
    tpb = S // NORM_TM
    n_a = A_COLS
    return pl.pallas_call(
        _norm_proj_kernel,
        grid=(N // NORM_TM,),
        in_specs=[pl.BlockSpec((NORM_TM, D), lambda i: (i, 0)),
                  pl.BlockSpec((1, D), lambda i: (0, 0)),
                  pl.BlockSpec((None, N_MOD, D), lambda i: (i // tpb, 0, 0)),
                  pl.BlockSpec((None, D, n_a), lambda i: (0, 0, 0))],
        out_specs=[pl.BlockSpec((NORM_TM, D), lambda i: (i, 0)),
                   pl.BlockSpec((NORM_TM, n_a), lambda i: (i, 0)),
                   pl.BlockSpec((NORM_TM, LANES), lambda i: (i, 0))],
        out_shape=[jax.ShapeDtypeStruct((N, D), BF16),
                   jax.ShapeDtypeStruct((N, n_a), BF16),
                   jax.ShapeDtypeStruct((N, LANES), F32)],
        compiler_params=_params("arbitrary"),
        name="norm1_proj_attn",
    )(x2, g.reshape(1, D), mod3, w_bf)


def _mm_kernel(a_ref, w_ref, o_ref):
    o_ref[...] = _dot(a_ref[...], w_ref[...]).astype(o_ref.dtype)


def _matmul(a, w, out_dtype, tm, tn, name):
    m, k = a.shape
    n = w.shape[1]
    return pl.pallas_call(
        _mm_kernel,
        grid=(m // tm, n // tn),
        in_specs=[pl.BlockSpec((tm, k), lambda i, j: (i, 0)),
                  pl.BlockSpec((k, tn), lambda i, j: (0, j))],
        out_specs=pl.BlockSpec((tm, tn), lambda i, j: (i, j)),
        out_shape=jax.ShapeDtypeStruct((m, n), out_dtype),
        compiler_params=_params("arbitrary", "arbitrary"),
        name=name,
    )(a, w)


A_COLS = 2 * ATTN_W + 2 * KV_W + LANES


def _spread_matrix(groups):
    r = lax.broadcasted_iota(jnp.int32, (groups * HEAD_DIM, groups * 4 * HEAD_DIM), 0)
    c = lax.broadcasted_iota(jnp.int32, (groups * HEAD_DIM, groups * 4 * HEAD_DIM), 1)
    d, cc = r & (HEAD_DIM - 1), c & (4 * HEAD_DIM - 1)
    hit = jnp.logical_and(jnp.right_shift(r, 6) == jnp.right_shift(c, 8),
                          jnp.logical_or(cc == d, cc == d + 3 * HEAD_DIM))
    return jnp.where(hit, 1.0, 0.0).astype(BF16)


def _attn_kernel(q_ref, iq_lo_ref, iq_hi_ref, tail_ref, ik_ref, k_ref, v_ref, o_ref,
                 s_ref, thr_ref, tie_ref, qs_ref, iqs_ref, ikp_ref, kp_ref, vaug_ref, acc_ref,
                 lg0_ref, a0_ref, m0_ref, lg1_ref, a1_ref, m1_ref):
    qi = pl.program_id(1)
    nkc = qi // (TK // TQ) + 1

    @pl.when(qi == 0)
    def _():
        spread_k = _spread_matrix(N_KV)
        spread_ik = _spread_matrix(2)[0:LANES, 0:2 * LANES]
        lane_v = lax.broadcasted_iota(jnp.int32, (TK, LANES), 1)
        ones = jnp.ones((TK, LANES), BF16)

        def fill(c, carry):
            off = pl.multiple_of(c * TK, TK)
            kp_ref[pl.ds(off, TK), :] = _dot(k_ref[pl.ds(off, TK), :], spread_k).astype(BF16)
            ikp_ref[pl.ds(off, TK), :] = _dot(ik_ref[pl.ds(off, TK), :], spread_ik).astype(BF16)
            for g in range(N_KV):
                src = v_ref[pl.ds(off, TK), LANES * (g // 2):LANES * (g // 2 + 1)]
                keep = (lane_v < 64) if g % 2 == 0 else (lane_v >= 64)
                vaug_ref[pl.ds(off, TK), LANES * g:LANES * (g + 1)] = jnp.where(keep, src, ones)
            return carry

        lax.fori_loop(0, S // TK, fill, 0)

    ident = jnp.where(lax.broadcasted_iota(jnp.int32, (TQ, LANES), 0)
                      == lax.broadcasted_iota(jnp.int32, (TQ, LANES), 1), 1.0, 0.0).astype(BF16)
    for j in range(8):
        iq_half = iq_lo_ref if j < 4 else iq_hi_ref
        iqs_ref[j * TQ:(j + 1) * TQ, :] = iq_half[:, LANES * (j % 4):LANES * (j % 4 + 1)]
        qs_ref[j * TQ:(j + 1) * TQ, 0:LANES] = q_ref[:, LANES * j:LANES * (j + 1)] * (HEAD_DIM ** -0.5)
        qs_ref[j * TQ:(j + 1) * TQ, LANES:2 * LANES] = ident

    w = tail_ref[...].T[IDX_DIM:IDX_DIM + IDX_HEADS, :] * ((IDX_HEADS ** -0.5) * (IDX_DIM ** -0.5))

    keyi = lax.broadcasted_iota(jnp.int32, (TK, TQ), 0)
    qryi = qi * TQ + lax.broadcasted_iota(jnp.int32, (TK, TQ), 1)

    def score_body(c, carry):
        lo, hi = carry
        off = pl.multiple_of(c * TK, TK)
        ik_even = ikp_ref[pl.ds(off, TK), 0:LANES]
        ik_odd = ikp_ref[pl.ds(off, TK), LANES:2 * LANES]
        acc = jnp.zeros((TK, TQ), F32)
        for jj in range(4):
            iq2 = iqs_ref[jj * 2 * TQ:(jj + 1) * 2 * TQ, :]
            for par, ik_par in ((0, ik_even), (1, ik_odd)):
                d = _dot_nt(ik_par, iq2)
                for t in range(2):
                    h = 2 * (2 * jj + t) + par
                    acc = acc + jnp.maximum(d[:, t * TQ:(t + 1) * TQ], 0.0) * w[h:h + 1, :]
        valid = (off + keyi) <= qryi
        s_ref[pl.ds(off, TK), :] = jnp.where(valid, acc, -jnp.inf)
        hi = jnp.maximum(hi, jnp.max(jnp.where(valid, acc, -jnp.inf), axis=0, keepdims=True))
        lo = jnp.minimum(lo, jnp.min(jnp.where(valid, acc, jnp.inf), axis=0, keepdims=True))
        return lo, hi

    lo0, hi0 = lax.fori_loop(0, nkc, score_body,
                             (jnp.full((1, TQ), jnp.inf, F32), jnp.full((1, TQ), -jnp.inf, F32)))

    def count_keys(pred):
        def body(c, cnt):
            off = pl.multiple_of(c * TK, TK)
            hit = jnp.where(pred(s_ref[pl.ds(off, TK), :], off + keyi), 1.0, 0.0)
            return cnt + hit.reshape(TK // CNT_ROWS, CNT_ROWS, TQ).sum(axis=0)

        cnt = lax.fori_loop(0, nkc, body, jnp.zeros((CNT_ROWS, TQ), F32))
        return jnp.sum(cnt, axis=0, keepdims=True)

    def count_ge(mid):
        return count_keys(lambda sc, _: sc >= mid)

    tie_ref[...] = jnp.full((8, TQ), S, jnp.int32)

    @pl.when(qi * TQ + TQ <= TOPK)
    def _():
        thr_ref[...] = jnp.full((8, TQ), jnp.finfo(F32).min, F32)

    @pl.when(qi * TQ + TQ > TOPK)
    def _():
        kf = float(TOPK)

        def step(lo, hi):
            mid = lo + (hi - lo) * 0.5
            cnt = count_ge(mid)
            ge = cnt >= kf
            eq = cnt == kf
            lo2 = jnp.where(ge, mid, lo)
            hi2 = jnp.where(eq, mid, jnp.where(ge, hi, mid))
            stuck = jnp.logical_or(mid <= lo, mid >= hi)
            return lo2, hi2, jnp.logical_and(lo2 < hi2, jnp.logical_not(stuck))

        def cond(st):
            it, _, _, active = st
            return jnp.logical_and(it < BISECT_MAX_IT, active > 0.0)

        def body(st):
            it, lo, hi, _ = st
            lo, hi, _ = step(lo, hi)
            lo, hi, act = step(lo, hi)
            return it + 2, lo, hi, jnp.max(jnp.where(act, 1.0, 0.0))

        _, lo, _, _ = lax.while_loop(cond, body, (jnp.int32(0), lo0, hi0, jnp.float32(1.0)))
        thr_ref[...] = jnp.broadcast_to(lo, (8, TQ))

        need = count_ge(lo) > kf

        @pl.when(jnp.max(jnp.where(need, 1.0, 0.0)) > 0.0)
        def _():
            keep = kf - count_keys(lambda sc, _: sc > lo)

            def index_step(_, st):
                below, upto = st
                mid = jnp.right_shift(below + upto, 1)
                ok = count_keys(lambda sc, ki: jnp.logical_and(sc == lo, ki <= mid)) >= keep
                return jnp.where(ok, below, mid), jnp.where(ok, mid, upto)

            _, upto = lax.fori_loop(0, TIE_STEPS, index_step,
                                    (jnp.full((1, TQ), -1, jnp.int32), jnp.full((1, TQ), S - 1, jnp.int32)))
            tie_ref[...] = jnp.broadcast_to(jnp.where(need, upto, S), (8, TQ))

    acc_ref[...] = jnp.zeros(acc_ref.shape, F32)
    thr = thr_ref[0:1, :]
    tie = tie_ref[0:1, :]
    nt = TK // LANES
    n_blk = 2 * N_KV
    bufs = ((lg0_ref, a0_ref, m0_ref), (lg1_ref, a1_ref, m1_ref))
    m1_ref[...] = jnp.full(m1_ref.shape, MASKED, F32)

    def pass_a(c, slot):
        lg_ref, a_ref, m_ref = bufs[slot]
        m_prev_ref = bufs[1 - slot][2]
        off = pl.multiple_of(c * TK, TK)
        sc = s_ref[pl.ds(off, TK), :]
        sel = jnp.logical_or(sc > thr, jnp.logical_and(sc == thr, (off + keyi) <= tie))
        bias_t = jnp.where(sel, 0.0, MASKED).astype(BF16)
        for blk in range(n_blk):
            g, par = blk // 2, blk % 2
            qg = qs_ref[g * 2 * TQ:(g + 1) * 2 * TQ, :]
            kk = kp_ref[pl.ds(off, TK), g * 256 + par * LANES:g * 256 + (par + 1) * LANES]
            lg = _dot_nt(qg, jnp.concatenate([kk, bias_t], axis=1))
            lg_ref[blk] = lg
            m_old = m_prev_ref[blk]
            m_new = jnp.maximum(m_old, jnp.max(lg, axis=1, keepdims=True))
            a_ref[blk] = jnp.exp(m_old - m_new)
            m_ref[blk] = m_new

    def pass_b(c, slot):
        lg_ref, a_ref, m_ref = bufs[slot]
        off = pl.multiple_of(c * TK, TK)
        for blk in range(n_blk):
            g = blk // 2
            m_new = m_ref[blk]
            p = jnp.concatenate([jnp.exp(lg_ref[blk, :, LANES * t:LANES * (t + 1)] - m_new)
                                 for t in range(nt)], axis=1).astype(BF16)
            va = vaug_ref[pl.ds(off, TK), LANES * g:LANES * (g + 1)]
            acc_ref[blk] = a_ref[blk] * acc_ref[blk] + _dot(p, va)

    pass_a(0, 0)

    def pair_body(i, carry):
        c = 2 * i
        pass_a(c + 1, 1)
        pass_b(c, 0)
        pass_a(c + 2, 0)
        pass_b(c + 1, 1)
        return carry

    lax.fori_loop(0, (nkc - 1) // 2, pair_body, 0)

    @pl.when(nkc % 2 == 0)
    def _():
        pass_a(nkc - 1, 1)
        pass_b(nkc - 2, 0)
        pass_b(nkc - 1, 1)

    @pl.when(nkc % 2 == 1)
    def _():
        pass_b(nkc - 1, 0)

    lane = lax.broadcasted_iota(jnp.int32, (TQ, LANES), 1)
    for g in range(N_KV):
        acc_e = acc_ref[2 * g]
        acc_o = acc_ref[2 * g + 1]
        o_even = acc_e / pltpu.roll(acc_e, 64, axis=1)
        o_odd = acc_o / pltpu.roll(acc_o, 64, axis=1)
        for pr in range(2):
            xa = o_even[pr * TQ:(pr + 1) * TQ]
            xb = o_odd[pr * TQ:(pr + 1) * TQ]
            if g % 2 == 0:
                left, right = xa, pltpu.roll(xb, 64, axis=1)
            else:
                left, right = pltpu.roll(xa, 64, axis=1), xb
            col = (2 * g + pr) * LANES
            o_ref[:, col:col + LANES] = jnp.where(lane < 64, left, right).astype(o_ref.dtype)


def _attention(a, tail):
    nq = S // TQ
    rowmap = lambda b, i: b * nq + i
    return pl.pallas_call(
        _attn_kernel,
        grid=(B, nq),
        in_specs=[pl.BlockSpec((TQ, ATTN_W), lambda b, i: (rowmap(b, i), 0)),
                  pl.BlockSpec((TQ, ATTN_W // 2), lambda b, i: (rowmap(b, i), 3)),
                  pl.BlockSpec((TQ, ATTN_W // 2), lambda b, i: (rowmap(b, i), 4)),
                  pl.BlockSpec((TQ, LANES), lambda b, i: (rowmap(b, i), 0)),
                  pl.BlockSpec((S, LANES), lambda b, i: (b, 20)),
                  pl.BlockSpec((S, KV_W), lambda b, i: (b, 4)),
                  pl.BlockSpec((S, KV_W), lambda b, i: (b, 5))],
        out_specs=pl.BlockSpec((TQ, ATTN_W), lambda b, i: (rowmap(b, i), 0)),
        out_shape=jax.ShapeDtypeStruct((N, ATTN_W), BF16),
        scratch_shapes=[pltpu.VMEM((S, TQ), F32),
                        pltpu.VMEM((8, TQ), F32),
                        pltpu.VMEM((8, TQ), jnp.int32),
                        pltpu.VMEM((8 * TQ, 2 * LANES), BF16),
                        pltpu.VMEM((8 * TQ, LANES), BF16),
                        pltpu.VMEM((S, 2 * LANES), BF16),
                        pltpu.VMEM((S, N_KV * 2 * LANES), BF16),
                        pltpu.VMEM((S, N_KV * LANES), BF16),
                        pltpu.VMEM((2 * N_KV, 2 * TQ, LANES), F32)]
                       + 2 * [pltpu.VMEM((2 * N_KV, 2 * TQ, TK), F32),
                              pltpu.VMEM((2 * N_KV, 2 * TQ, LANES), F32),
                              pltpu.VMEM((2 * N_KV, 2 * TQ, LANES), F32)],

        compiler_params=_params("arbitrary", "arbitrary"),
        name="sparse_attn",
    )(a, a, a, tail, a, a, a)


CONV_TS = 512


def _conv_kernel(bg_ref, cg_ref, xc_ref, cw_ref, o_ref, carry_ref):
    i = pl.program_id(0)

    @pl.when(i % (S // CONV_TS) == 0)
    def _():
        carry_ref[...] = jnp.zeros(carry_ref.shape, F32)

    u = cg_ref[...] * xc_ref[...]
    prev = carry_ref[...]
    row = lax.broadcasted_iota(jnp.int32, u.shape, 0)
    u1 = jnp.where(row == 0, prev[7:8, :], pltpu.roll(u, 1, axis=0))
    u2 = jnp.where(row == 0, prev[6:7, :], jnp.where(row == 1, prev[7:8, :], pltpu.roll(u, 2, axis=0)))
    y = cw_ref[0:1, :] * u2 + cw_ref[1:2, :] * u1 + cw_ref[2:3, :] * u
    o_ref[...] = (bg_ref[...] * y).astype(o_ref.dtype)
    carry_ref[...] = u[CONV_TS - 8:CONV_TS, :]


def _short_conv(pb, conv_w):
    return pl.pallas_call(
        _conv_kernel,
        grid=(N // CONV_TS,),
        in_specs=[pl.BlockSpec((CONV_TS, CONV_W), lambda i: (i, 0)),
                  pl.BlockSpec((CONV_TS, CONV_W), lambda i: (i, 1)),
                  pl.BlockSpec((CONV_TS, CONV_W), lambda i: (i, 2)),
                  pl.BlockSpec((3, CONV_W), lambda i: (0, 0))],
        out_specs=pl.BlockSpec((CONV_TS, CONV_W), lambda i: (i, 0)),
        out_shape=jax.ShapeDtypeStruct((N, CONV_W), BF16),
        scratch_shapes=[pltpu.VMEM((8, CONV_W), F32)],
        compiler_params=_params("arbitrary"),
        name="short_conv",
    )(pb, pb, pb, conv_w)


MG_TM = 1024
MG_TN = 512
CM_TM = 512
CM_TN = 1024


def _conv_merge_kernel(oa_ref, bg_ref, cg_ref, xc_ref, cw_ref, wa_ref, wb_ref, ga_ref, gb_ref, o_ref,
                       zb_ref, carry_ref):
    i, j = pl.program_id(0), pl.program_id(1)

    @pl.when(j == 0)
    def _():
        @pl.when(i % (S // CM_TM) == 0)
        def _():
            carry_ref[...] = jnp.zeros(carry_ref.shape, F32)

        u = cg_ref[...] * xc_ref[...]
        prev = carry_ref[...]
        row = lax.broadcasted_iota(jnp.int32, u.shape, 0)
        u1 = jnp.where(row == 0, prev[7:8, :], pltpu.roll(u, 1, axis=0))
        u2 = jnp.where(row == 0, prev[6:7, :], jnp.where(row == 1, prev[7:8, :], pltpu.roll(u, 2, axis=0)))
        y = cw_ref[0:1, :] * u2 + cw_ref[1:2, :] * u1 + cw_ref[2:3, :] * u
        zb_ref[...] = (bg_ref[...] * y).astype(zb_ref.dtype)
        carry_ref[...] = u[CM_TM - 8:CM_TM, :]

    ya = _dot(oa_ref[...], wa_ref[...])
    yb = _dot(zb_ref[...], wb_ref[...])
    merged = jax.nn.sigmoid(ga_ref[...]) * ya + jax.nn.sigmoid(gb_ref[...]) * yb
    o_ref[...] = merged.astype(o_ref.dtype)


def _conv_merge(oa, pb, conv_w, wa, wb):
    ga0 = (3 * CONV_W) // CM_TN
    gb0 = (3 * CONV_W + D) // CM_TN
    return pl.pallas_call(
        _conv_merge_kernel,
        grid=(N // CM_TM, D // CM_TN),
        in_specs=[pl.BlockSpec((CM_TM, ATTN_W), lambda i, j: (i, 0)),
                  pl.BlockSpec((CM_TM, CONV_W), lambda i, j: (i, 0)),
                  pl.BlockSpec((CM_TM, CONV_W), lambda i, j: (i, 1)),
                  pl.BlockSpec((CM_TM, CONV_W), lambda i, j: (i, 2)),
                  pl.BlockSpec((3, CONV_W), lambda i, j: (0, 0)),
                  pl.BlockSpec((ATTN_W, CM_TN), lambda i, j: (0, j)),
                  pl.BlockSpec((CONV_W, CM_TN), lambda i, j: (0, j)),
                  pl.BlockSpec((CM_TM, CM_TN), lambda i, j: (i, ga0 + j)),
                  pl.BlockSpec((CM_TM, CM_TN), lambda i, j: (i, gb0 + j))],
        out_specs=pl.BlockSpec((CM_TM, CM_TN), lambda i, j: (i, j)),
        out_shape=jax.ShapeDtypeStruct((N, D), BF16),
        scratch_shapes=[pltpu.VMEM((CM_TM, CONV_W), BF16), pltpu.VMEM((8, CONV_W), F32)],
        compiler_params=_params("arbitrary", "arbitrary"),
        name="conv_gated_merge",
    )(oa, pb, pb, pb, conv_w, wa, wb, pb, pb)


def _wo_kernel(m_ref, w_ref, x_ref, mod_ref, o_ref):
    o_ref[...] = x_ref[...] + mod_ref[2:3, :] * _dot(m_ref[...], w_ref[...])


WO_TN = 1024


def _out_proj(merged, wo, x2, mod3):
    tpb = S // MG_TM
    return pl.pallas_call(
        _wo_kernel,
        grid=(N // MG_TM, D // WO_TN),
        in_specs=[pl.BlockSpec((MG_TM, D), lambda i, j: (i, 0)),
                  pl.BlockSpec((D, WO_TN), lambda i, j: (0, j)),
                  pl.BlockSpec((MG_TM, WO_TN), lambda i, j: (i, j)),
                  pl.BlockSpec((None, N_MOD, WO_TN), lambda i, j: (i // tpb, 0, j))],
        out_specs=pl.BlockSpec((MG_TM, WO_TN), lambda i, j: (i, j)),
        out_shape=jax.ShapeDtypeStruct((N, D), F32),
        compiler_params=_params("arbitrary", "arbitrary"),
        name="out_proj",
    )(merged, wo, x2, mod3)


RT_TM = 256


def _router_kernel(x_ref, g_ref, mod_ref, wr_ref, br_ref,
                   h_ref, mi_ref, mw_ref, cnt_ref, carry_ref):
    i = pl.program_id(0)

    @pl.when(i == 0)
    def _():
        carry_ref[...] = jnp.zeros(carry_ref.shape, F32)

    h = _modulated_norm(x_ref[...], g_ref[...], mod_ref[3:4, :], mod_ref[4:5, :])
    h_ref[...] = h
    lg = _dot(h.astype(BF16), wr_ref[...]) + br_ref[...]
    lane = lax.broadcasted_iota(jnp.int32, lg.shape, 1)
    ninf = -jnp.inf

    gl = jnp.where(lane < N_GROUPS, lg, ninf)
    gm = jnp.max(gl, axis=1, keepdims=True)
    g_sel = jnp.min(jnp.where(gl == gm, lane, LANES), axis=1, keepdims=True)
    p_g = 1.0 / jnp.sum(jnp.exp(gl - gm), axis=1, keepdims=True)

    grp = jnp.right_shift(lane - N_GROUPS, 3)
    in_grp = jnp.logical_and(jnp.logical_and(lane >= N_GROUPS, lane < N_GROUPS + N_EXP), grp == g_sel)
    el = jnp.where(in_grp, lg, ninf)
    m1 = jnp.max(el, axis=1, keepdims=True)
    i1 = jnp.min(jnp.where(el == m1, lane, LANES), axis=1, keepdims=True)
    el2 = jnp.where(lane == i1, ninf, el)
    m2 = jnp.max(el2, axis=1, keepdims=True)
    i2 = jnp.min(jnp.where(el2 == m2, lane, LANES), axis=1, keepdims=True)
    e = jnp.exp(m2 - m1)
    w1 = p_g * (1.0 / (1.0 + e))
    w2 = p_g * (e / (1.0 + e))
    e1 = i1 - N_GROUPS
    e2 = i2 - N_GROUPS

    oh1 = lane == e1
    oh2 = lane == e2
    used = jnp.where(jnp.logical_or(oh1, oh2), 1.0, 0.0)
    ri = lax.broadcasted_iota(jnp.int32, (RT_TM, RT_TM), 0)
    ci = lax.broadcasted_iota(jnp.int32, (RT_TM, RT_TM), 1)
    lower = jnp.where(ci < ri, 1.0, 0.0).astype(BF16)
    before = _dot(lower, used.astype(BF16)) + carry_ref[...]
    r1 = jnp.sum(jnp.where(oh1, before, 0.0), axis=1, keepdims=True).astype(jnp.int32)
    r2 = jnp.sum(jnp.where(oh2, before, 0.0), axis=1, keepdims=True).astype(jnp.int32)
    total = carry_ref[...] + jnp.sum(used, axis=0, keepdims=True)
    carry_ref[...] = total
    cnt_ref[...] = total

    zero = jnp.zeros_like(lane)
    mi_ref[...] = jnp.where(lane == 0, e1, jnp.where(lane == 1, e2,
                            jnp.where(lane == 2, r1, jnp.where(lane == 3, r2, zero))))
    mw_ref[...] = jnp.where(lane == 0, w1, jnp.where(lane == 1, w2, 0.0))


def _router(x1, g, mod3, wr, br):
    tpb = S // RT_TM
    return pl.pallas_call(
        _router_kernel,
        grid=(N // RT_TM,),
        in_specs=[pl.BlockSpec((RT_TM, D), lambda i: (i, 0)),
                  pl.BlockSpec((1, D), lambda i: (0, 0)),
                  pl.BlockSpec((None, N_MOD, D), lambda i: (i // tpb, 0, 0)),
                  pl.BlockSpec((D, LANES), lambda i: (0, 0)),
                  pl.BlockSpec((1, LANES), lambda i: (0, 0))],
        out_specs=[pl.BlockSpec((RT_TM, D), lambda i: (i, 0)),
                   pl.BlockSpec((RT_TM, LANES), lambda i: (i, 0)),
                   pl.BlockSpec((RT_TM, LANES), lambda i: (i, 0)),
                   pl.BlockSpec((1, LANES), lambda i: (0, 0))],
        out_shape=[jax.ShapeDtypeStruct((N, D), F32),
                   jax.ShapeDtypeStruct((N, LANES), jnp.int32),
                   jax.ShapeDtypeStruct((N, LANES), F32),
                   jax.ShapeDtypeStruct((1, LANES), F32)],
        scratch_shapes=[pltpu.VMEM((1, LANES), F32)],
        compiler_params=_params("arbitrary"),
        name="norm2_router",
    )(x1, g.reshape(1, D), mod3, wr, br)


DP_TM = 256


def _row_copy(src_ref, src_row, dst_ref, dst_row, sem):
    return pltpu.make_async_copy(src_ref.at[pl.ds(src_row, 1), :], dst_ref.at[pl.ds(dst_row, 1), :], sem)


def _dispatch_kernel(slot_ref, h_ref, xs_in_ref, xs_ref, sem):
    del xs_in_ref
    base = pl.program_id(0) * DP_TM

    for r in range(DP_TM):
        for k in range(2):
            _row_copy(h_ref, r, xs_ref, slot_ref[(base + r) * 2 + k], sem).start(priority=k)

    def drain(r, carry):
        for _ in range(2):
            _row_copy(h_ref, 0, xs_ref, 0, sem).wait()
        return carry

    lax.fori_loop(0, DP_TM, drain, 0, unroll=8)


def _dispatch(slot, h2, xs0):
    grid_spec = pltpu.PrefetchScalarGridSpec(
        num_scalar_prefetch=1,
        grid=(N // DP_TM,),
        in_specs=[pl.BlockSpec((DP_TM, D), lambda i, *_: (i, 0)),
                  pl.BlockSpec(memory_space=pl.ANY)],
        out_specs=pl.BlockSpec(memory_space=pl.ANY),
        scratch_shapes=[pltpu.SemaphoreType.DMA(())],
    )
    return pl.pallas_call(
        _dispatch_kernel,
        grid_spec=grid_spec,
        out_shape=jax.ShapeDtypeStruct((XS_ROWS, D), F32),
        input_output_aliases={2: 0},
        compiler_params=_params("arbitrary"),
        name="moe_dispatch",
    )(slot, h2, xs0)


def _expert_kernel(ce_ref, nc_ref, x_ref, wg_ref, wu_ref, wd_ref, y_ref, wg_s, wu_s, wd_s):
    c = pl.program_id(0)
    valid = c < nc_ref[0]
    new_expert = jnp.logical_or(c == 0, ce_ref[c] != ce_ref[jnp.maximum(c - 1, 0)])

    @pl.when(jnp.logical_and(valid, new_expert))
    def _():
        wg_s[...] = wg_ref[...].astype(BF16)
        wu_s[...] = wu_ref[...].astype(BF16)
        wd_s[...] = wd_ref[...].astype(BF16)

    @pl.when(valid)
    def _():
        x = x_ref[...].astype(BF16)
        a = _dot(x, wg_s[...])
        u = _dot(x, wu_s[...])
        act = (a * jax.nn.sigmoid(a)) * u
        y_ref[...] = _dot(act.astype(BF16), wd_s[...])


def _experts(ce, nc, xs, wg, wu, wd):
    rows = lambda c, ce, nc: (jnp.minimum(c, nc[0] - 1), 0)
    wmap = lambda c, ce, nc: (ce[c], 0, 0)
    grid_spec = pltpu.PrefetchScalarGridSpec(
        num_scalar_prefetch=2,
        grid=(MAX_CHUNKS,),
        in_specs=[pl.BlockSpec((CH, D), rows),
                  pl.BlockSpec((None, D, FF), wmap),
                  pl.BlockSpec((None, D, FF), wmap),
                  pl.BlockSpec((None, FF, D), wmap)],
        out_specs=pl.BlockSpec((CH, D), rows),
        scratch_shapes=[pltpu.VMEM((D, FF), BF16), pltpu.VMEM((D, FF), BF16), pltpu.VMEM((FF, D), BF16)],
    )
    return pl.pallas_call(
        _expert_kernel,
        grid_spec=grid_spec,
        out_shape=jax.ShapeDtypeStruct((XS_ROWS, D), F32),
        input_output_aliases={2: 0},
        compiler_params=_params("arbitrary"),
        name="moe_experts",
    )(ce, nc, xs, wg, wu, wd)


CB_TM = 256


def _combine_kernel(slot_ref, y_ref, x_ref, mw_ref, mod_ref, g_ref, o_ref, ybuf, sem):
    i = pl.program_id(0)
    cur = i % 2

    def issue(tile, buf):
        for r in range(CB_TM):
            for k in range(2):
                _row_copy(y_ref, slot_ref[(tile * CB_TM + r) * 2 + k], ybuf.at[buf, k], r,
                          sem.at[buf]).start(priority=k)

    @pl.when(i == 0)
    def _():
        issue(0, 0)

    @pl.when(i + 1 < pl.num_programs(0))
    def _():
        issue(i + 1, 1 - cur)

    def drain(r, carry):
        for k in range(2):
            _row_copy(y_ref, 0, ybuf.at[cur, k], 0, sem.at[cur]).wait()
        return carry

    lax.fori_loop(0, CB_TM, drain, 0, unroll=8)

    moe = mw_ref[:, 0:1] * ybuf[cur, 0] + mw_ref[:, 1:2] * ybuf[cur, 1]
    x = x_ref[...] + mod_ref[5:6, :] * moe
    r = lax.rsqrt(jnp.mean(x * x, axis=-1, keepdims=True) + EPS)
    o_ref[...] = (x * r) * g_ref[...]


def _combine(slot, y, x1, mw, mod3, g):
    tpb = S // CB_TM
    grid_spec = pltpu.PrefetchScalarGridSpec(
        num_scalar_prefetch=1,
        grid=(N // CB_TM,),
        in_specs=[pl.BlockSpec(memory_space=pl.ANY),
                  pl.BlockSpec((CB_TM, D), lambda i, *_: (i, 0)),
                  pl.BlockSpec((CB_TM, LANES), lambda i, *_: (i, 0)),
                  pl.BlockSpec((None, N_MOD, D), lambda i, *_: (i // tpb, 0, 0)),
                  pl.BlockSpec((1, D), lambda i, *_: (0, 0))],
        out_specs=pl.BlockSpec((CB_TM, D), lambda i, *_: (i, 0)),
        scratch_shapes=[pltpu.VMEM((2, 2, CB_TM, D), F32), pltpu.SemaphoreType.DMA((2,))],
    )
    return pl.pallas_call(
        _combine_kernel,
        grid_spec=grid_spec,
        out_shape=jax.ShapeDtypeStruct((N, D), F32),
        compiler_params=_params("arbitrary"),
        name="moe_combine_norm",
    )(slot, y, x1, mw, mod3, g.reshape(1, D))


def kernel(x, c, w_ada, b_ada, g_norm_mix, w_in, conv_w, w_a_up, w_b_out, w_o, g_norm_ffn,
           w_rg, b_rg, w_re, b_re, w_e_gate, w_e_up, w_e_down, g_norm_final):
    x2 = x.reshape(N, D)
    mod3 = _ada(c, w_ada[0], b_ada[0]).reshape(B, N_MOD, D)

    o_bg = 2640
    w_bf = w_in.astype(BF16)
    w_b = w_bf[0, :, o_bg:]

    h, pa, tail = _norm_proj(x2, g_norm_mix[0], mod3, w_bf)
    pb = _matmul(h, w_b, F32, 1024, 1024, "proj_conv_gates")

    oa = _attention(pa, tail)
    merged = _conv_merge(oa, pb, conv_w[0], w_a_up[0].astype(BF16), w_b_out[0].astype(BF16))
    x1 = _out_proj(merged, w_o[0].astype(BF16), x2, mod3)

    wr = jnp.pad(jnp.concatenate([w_rg[0], w_re[0]], axis=1),
                 ((0, 0), (0, LANES - N_GROUPS - N_EXP))).astype(BF16)
    br = jnp.pad(jnp.concatenate([b_rg[0], b_re[0]]), (0, LANES - N_GROUPS - N_EXP)).reshape(1, LANES)
    h2, mi, mw, cnt = _router(x1, g_norm_ffn[0], mod3, wr, br)

    counts = cnt[0, :N_EXP].astype(jnp.int32)
    nch = (counts + CH - 1) // CH
    cum = jnp.cumsum(nch)
    offs = ((cum - nch) * CH).astype(jnp.int32)
    total = cum[-1]
    cidx = jnp.minimum(jnp.arange(MAX_CHUNKS, dtype=jnp.int32), total - 1)
    ce = jnp.minimum(jnp.sum((cum[None, :] <= cidx[:, None]).astype(jnp.int32), axis=1), N_EXP - 1)
    nc = total.reshape(1).astype(jnp.int32)
    seg = jnp.sum(jnp.where(mi[:, 0:2, None] == jnp.arange(N_EXP, dtype=jnp.int32), offs, 0), axis=-1)
    slot = (seg + mi[:, 2:4]).reshape(-1)

    xs = _dispatch(slot, h2, jnp.zeros((XS_ROWS, D), F32))
    y = _experts(ce, nc, xs, w_e_gate[0], w_e_up[0], w_e_down[0])
    out = _combine(slot, y, x1, mw, mod3, g_norm_final)
    return out.reshape(B, S, D)
```
